```python
import jax, jax.numpy as jnp
from jax import lax
import numpy as np

D_MODEL = 2048
BATCH = 4
SEQ = 4096
DEPTH = 1
DEC_BATCH = 32
DEC_SEQ = 64
PAST_LEN = 1024

CHUNK = 64
Q_BLOCK = 128
NORM_EPS = 1e-6
D_RNN = 2048
RNN_BLOCKS = 16
RNN_BLOCK_W = D_RNN // RNN_BLOCKS
CONV_W = 4
RG_C = 8.0
MLA_HEADS = 16
QK_NOPE = 128
ROPE_DIM = 64
V_DIM = 128
Q_LORA = 512
KV_LORA = 512
ROPE_BASE = 10000.0
MLA_SCALE = (QK_NOPE + ROPE_DIM) ** -0.5
N_MEM = 256
X_HEADS = 4
X_HEAD_DIM = 128
N_EXPERTS = 32
TOP_K = 4
D_FF = 2048
SWIGLU_LIMIT = 7.0
SWIGLU_ALPHA = 1.702
MOE_BLOCK = 256
D_IN = 2 * D_RNN + Q_LORA + KV_LORA + ROPE_DIM + 2 * D_MODEL
IN_SPLIT_POINTS = (D_RNN, 2 * D_RNN, 2 * D_RNN + Q_LORA, 2 * D_RNN + Q_LORA + KV_LORA,
                   2 * D_RNN + Q_LORA + KV_LORA + ROPE_DIM, 2 * D_RNN + Q_LORA + KV_LORA + ROPE_DIM + D_MODEL)

kernel_name = 'hybrid_rglru_mla_moe_stream_step'


def rmsnorm(x, g):
    xf = x.astype(jnp.float32)
    y = xf * lax.rsqrt(jnp.mean(xf * xf, axis=-1, keepdims=True) + NORM_EPS)
    return (y * g.astype(jnp.float32)).astype(x.dtype)


def rope(x, pos):
    half = ROPE_DIM // 2
    freq = ROPE_BASE ** (-jnp.arange(half, dtype=jnp.float32) / half)
    ang = pos.astype(jnp.float32)[:, None] * freq[None, :]
    ang = ang.reshape(ang.shape[:1] + (1,) * (x.ndim - 3) + (half,))
    cos, sin = jnp.cos(ang), jnp.sin(ang)
    xf = x.astype(jnp.float32)
    x1, x2 = xf[..., :half], xf[..., half:]
    return jnp.concatenate([x1 * cos - x2 * sin, x1 * sin + x2 * cos], axis=-1).astype(x.dtype)


def causal_conv(xr, buf, w, b):
    s = xr.shape[1]
    xp = jnp.concatenate([buf.astype(xr.dtype), xr], axis=1)
    out = b
    for k in range(CONV_W):
        out = out + xp[:, k:k + s] * w[k]
    return out, xp[:, -(CONV_W - 1):]


def rg_lru(xc, h0, w_a, b_a, w_x, b_x, lam):
    bsz, s, _ = xc.shape
    xh = xc.reshape(bsz, s, RNN_BLOCKS, RNN_BLOCK_W)
    r = jax.nn.sigmoid((jnp.einsum('bshi,hij->bshj', xh, w_a) + b_a).astype(jnp.float32)).reshape(bsz, s, D_RNN)
    i = jax.nn.sigmoid((jnp.einsum('bshi,hij->bshj', xh, w_x) + b_x).astype(jnp.float32)).reshape(bsz, s, D_RNN)
    log_a = -RG_C * r * jax.nn.softplus(-lam.astype(jnp.float32))
    a = jnp.exp(log_a)
    u = jnp.sqrt(-jnp.expm1(2.0 * log_a)) * (i * xc.astype(jnp.float32))
    u = u.at[:, 0].add(a[:, 0] * h0.astype(jnp.float32))

    def combine(left, right):
        return (left[0] * right[0], right[0] * left[1] + right[1])

    _, h = lax.associative_scan(combine, (a, u), axis=1)
    return h, h[:, -1]


def mla_attend(q_nope, q_rope, k_nope, k_rope, v, q_pos, k_pos):
    s = (jnp.einsum('bqhd,bkhd->bhqk', q_nope, k_nope)
         + jnp.einsum('bqhr,bkr->bhqk', q_rope, k_rope)).astype(jnp.float32) * MLA_SCALE
    visible = (k_pos[None, :] // CHUNK) <= (q_pos[:, None] // CHUNK)
    s = jnp.where(visible[None, None], s, -1e30)
    p = jax.nn.softmax(s, axis=-1).astype(v.dtype)
    return jnp.einsum('bhqk,bkhd->bqhd', p, v)


def memory_kv(mem, g, w_xk, w_xv):
    mn = rmsnorm(mem, g)
    return jnp.einsum('bmd,dhe->bmhe', mn, w_xk), jnp.einsum('bmd,dhe->bmhe', mn, w_xv)


def cross_attend(hn, mem_k, mem_v, w_xq, w_xo):
    q = jnp.einsum('bsd,dhe->bshe', hn, w_xq)
    s = jnp.einsum('bshe,bmhe->bhsm', q, mem_k.astype(q.dtype)).astype(jnp.float32) * (X_HEAD_DIM ** -0.5)
    p = jax.nn.softmax(s, axis=-1).astype(hn.dtype)
    o = jnp.einsum('bhsm,bmhe->bshe', p, mem_v.astype(hn.dtype))
    return jnp.einsum('bshe,hed->bsd', o, w_xo)


def moe_ffn(h, router_w, router_b, w_gate, b_gate, w_up, b_up, w_down, b_down):
    lead = h.shape[:-1]
    t = h.reshape(-1, D_MODEL)
    m = t.shape[0]
    logits = (t @ router_w).astype(jnp.float32) + router_b.astype(jnp.float32)
    top_v, top_i = lax.top_k(logits, TOP_K)
    top_w = jax.nn.softmax(top_v, axis=-1)
    n_assign = m * TOP_K
    e_flat = top_i.reshape(-1)
    tok_flat = jnp.repeat(jnp.arange(m, dtype=jnp.int32), TOP_K)
    w_flat = top_w.reshape(-1)
    order = jnp.argsort(e_flat)
    e_sorted = e_flat[order]
    counts = jnp.bincount(e_flat, length=N_EXPERTS)
    padded = (counts + MOE_BLOCK - 1) // MOE_BLOCK * MOE_BLOCK
    start = jnp.cumsum(counts) - counts
    pend = jnp.cumsum(padded)
    pstart = pend - padded
    dest = pstart[e_sorted] + jnp.arange(n_assign) - start[e_sorted]
    n_blocks = -(-n_assign // MOE_BLOCK) + N_EXPERTS
    rows = n_blocks * MOE_BLOCK
    row_tok = jnp.zeros((rows,), jnp.int32).at[dest].set(tok_flat[order])
    row_w = jnp.zeros((rows,), jnp.float32).at[dest].set(w_flat[order])
    blk_e = jnp.minimum(jnp.searchsorted(pend, jnp.arange(n_blocks) * MOE_BLOCK, side='right'), N_EXPERTS - 1)
    xb = t[row_tok].reshape(n_blocks, MOE_BLOCK, D_MODEL)

    def expert_block(args):
        xblk, e = args
        g = jnp.minimum(xblk @ w_gate[e] + b_gate[e], SWIGLU_LIMIT)
        u = jnp.clip(xblk @ w_up[e] + b_up[e], -SWIGLU_LIMIT, SWIGLU_LIMIT)
        act = (u + 1.0) * g * jax.nn.sigmoid(SWIGLU_ALPHA * g)
        return act @ w_down[e] + b_down[e]

    yb = lax.map(expert_block, (xb, blk_e)).reshape(rows, D_MODEL)
    y = jnp.zeros((m, D_MODEL), jnp.float32).at[row_tok].add(yb.astype(jnp.float32) * row_w[:, None])
    return y.astype(h.dtype).reshape(lead + (D_MODEL,))


def layer(x, pos, conv_buf, h0, past_ckv, past_krope, mem_k, mem_v, p):
    bsz, s, _ = x.shape
    hn = rmsnorm(x, p['norm_mix_g'])
    z = hn @ p['w_in'] + p['b_in']
    xr, yr, cq, ckv, kr, g_rnn, g_mla = jnp.split(z, IN_SPLIT_POINTS, axis=-1)
    xc, new_buf = causal_conv(xr, conv_buf, p['rg_conv_w'], p['rg_conv_b'])
    h, h_last = rg_lru(xc, h0, p['rg_a_w'], p['rg_a_b'], p['rg_x_w'], p['rg_x_b'], p['rg_lambda'])
    o_rnn = (h.astype(x.dtype) * jax.nn.gelu(yr)) @ p['w_rnn_proj']
    cq = rmsnorm(cq, p['mla_q_norm_g'])
    q_nope = jnp.einsum('bsc,chd->bshd', cq, p['mla_w_uq_nope'])
    q_rope = rope(jnp.einsum('bsc,chr->bshr', cq, p['mla_w_uq_rope']), pos)
    c_kv = rmsnorm(ckv, p['mla_kv_norm_g'])
    k_rope = rope(kr, pos)
    if past_ckv is None:
        kv_lat, kv_rope, k_pos = c_kv, k_rope, pos
    else:
        kv_lat = jnp.concatenate([past_ckv.astype(c_kv.dtype), c_kv], axis=1)
        kv_rope = jnp.concatenate([past_krope.astype(k_rope.dtype), k_rope], axis=1)
        k_pos = jnp.arange(kv_lat.shape[1], dtype=jnp.int32)
    k_nope = jnp.einsum('bkc,chd->bkhd', kv_lat, p['mla_w_uk'])
    v = jnp.einsum('bkc,chd->bkhd', kv_lat, p['mla_w_uv'])
    if past_ckv is None:
        outs = []
        for blk in range(s // Q_BLOCK):
            q0, q1 = blk * Q_BLOCK, (blk + 1) * Q_BLOCK
            outs.append(mla_attend(q_nope[:, q0:q1], q_rope[:, q0:q1], k_nope[:, :q1], kv_rope[:, :q1],
                                   v[:, :q1], pos[q0:q1], k_pos[:q1]))
        o = jnp.concatenate(outs, axis=1)
    else:
        o = mla_attend(q_nope, q_rope, k_nope, kv_rope, v, pos, k_pos)
    o_mla = o.reshape(bsz, s, MLA_HEADS * V_DIM) @ p['w_mla_proj']
    merged = jax.nn.sigmoid(g_rnn) * o_rnn + jax.nn.sigmoid(g_mla) * o_mla
    x = x + merged @ p['w_out']
    x = x + cross_attend(rmsnorm(x, p['norm_x_g']), mem_k, mem_v, p['w_xq'], p['w_xo'])
    x = x + moe_ffn(rmsnorm(x, p['norm_ffn_g']), p['router_w'], p['router_b'], p['w_gate'], p['b_gate'],
                    p['w_up'], p['b_up'], p['w_down'], p['b_down'])
    return x, new_buf, h_last, c_kv, k_rope


def setup_inputs(seed: int = 0) -> dict:
    key = jax.random.key(seed)
    ks = iter(jax.random.split(key, 64))

    def nrm(shape, scale):
        return jax.random.normal(next(ks), shape, jnp.float32) * scale

    def gain(shape):
        return 1.0 + 0.01 * jax.random.normal(next(ks), shape, jnp.float32)

    u = jax.random.uniform(next(ks), (DEPTH, D_RNN), jnp.float32, 0.9, 0.999)
    sig = u ** (1.0 / RG_C)
    rg_lambda = jnp.log(sig) - jnp.log1p(-sig)
    return {
        'x_prompt': nrm((BATCH, SEQ, D_MODEL), 1.0),
        'x_sample': nrm((DEC_BATCH, DEC_SEQ, D_MODEL), 1.0),
        'mem_prompt': nrm((BATCH, N_MEM, D_MODEL), 1.0),
        'cache_mla_ckv': nrm((DEPTH, DEC_BATCH, PAST_LEN, KV_LORA), 1.0),
        'cache_mla_krope': nrm((DEPTH, DEC_BATCH, PAST_LEN, ROPE_DIM), 1.0),
        'cache_mem_k': nrm((DEPTH, DEC_BATCH, N_MEM, X_HEADS, X_HEAD_DIM), 1.0),
        'cache_mem_v': nrm((DEPTH, DEC_BATCH, N_MEM, X_HEADS, X_HEAD_DIM), 1.0),
        'state_rg_h': nrm((DEPTH, DEC_BATCH, D_RNN), 0.5),
        'state_rg_conv': nrm((DEPTH, DEC_BATCH, CONV_W - 1, D_RNN), 1.0),
        'norm_mix_g': gain((DEPTH, D_MODEL)),
        'w_in': nrm((DEPTH, D_MODEL, D_IN), D_MODEL ** -0.5),
        'b_in': nrm((DEPTH, D_IN), 0.01),
        'rg_conv_w': nrm((DEPTH, CONV_W, D_RNN), CONV_W ** -0.5),
        'rg_conv_b': nrm((DEPTH, D_RNN), 0.01),
        'rg_a_w': nrm((DEPTH, RNN_BLOCKS, RNN_BLOCK_W, RNN_BLOCK_W), RNN_BLOCK_W ** -0.5),
        'rg_a_b': nrm((DEPTH, RNN_BLOCKS, RNN_BLOCK_W), 0.01),
        'rg_x_w': nrm((DEPTH, RNN_BLOCKS, RNN_BLOCK_W, RNN_BLOCK_W), RNN_BLOCK_W ** -0.5),
        'rg_x_b': nrm((DEPTH, RNN_BLOCKS, RNN_BLOCK_W), 0.01),
        'rg_lambda': rg_lambda,
        'w_rnn_proj': nrm((DEPTH, D_RNN, D_MODEL), D_RNN ** -0.5),
        'mla_q_norm_g': gain((DEPTH, Q_LORA)),
        'mla_w_uq_nope': nrm((DEPTH, Q_LORA, MLA_HEADS, QK_NOPE), Q_LORA ** -0.5),
        'mla_w_uq_rope': nrm((DEPTH, Q_LORA, MLA_HEADS, ROPE_DIM), Q_LORA ** -0.5),
        'mla_kv_norm_g': gain((DEPTH, KV_LORA)),
        'mla_w_uk': nrm((DEPTH, KV_LORA, MLA_HEADS, QK_NOPE), KV_LORA ** -0.5),
        'mla_w_uv': nrm((DEPTH, KV_LORA, MLA_HEADS, V_DIM), KV_LORA ** -0.5),
        'w_mla_proj': nrm((DEPTH, MLA_HEADS * V_DIM, D_MODEL), (MLA_HEADS * V_DIM) ** -0.5),
        'w_out': nrm((DEPTH, D_MODEL, D_MODEL), D_MODEL ** -0.5),
        'norm_x_g': gain((DEPTH, D_MODEL)),
        'norm_mem_g': gain((DEPTH, D_MODEL)),
        'w_xq': nrm((DEPTH, D_MODEL, X_HEADS, X_HEAD_DIM), D_MODEL ** -0.5),
        'w_xk': nrm((DEPTH, D_MODEL, X_HEADS, X_HEAD_DIM), D_MODEL ** -0.5),
        'w_xv': nrm((DEPTH, D_MODEL, X_HEADS, X_HEAD_DIM), D_MODEL ** -0.5),
        'w_xo': nrm((DEPTH, X_HEADS, X_HEAD_DIM, D_MODEL), (X_HEADS * X_HEAD_DIM) ** -0.5),
        'norm_ffn_g': gain((DEPTH, D_MODEL)),
        'router_w': nrm((DEPTH, D_MODEL, N_EXPERTS), D_MODEL ** -0.5),
        'router_b': nrm((DEPTH, N_EXPERTS), 0.01),
        'w_gate': nrm((DEPTH, N_EXPERTS, D_MODEL, D_FF), D_MODEL ** -0.5),
        'b_gate': nrm((DEPTH, N_EXPERTS, D_FF), 0.01),
        'w_up': nrm((DEPTH, N_EXPERTS, D_MODEL, D_FF), D_MODEL ** -0.5),
        'b_up': nrm((DEPTH, N_EXPERTS, D_FF), 0.01),
        'w_down': nrm((DEPTH, N_EXPERTS, D_FF, D_MODEL), D_FF ** -0.5),
        'b_down': nrm((DEPTH, N_EXPERTS, D_MODEL), 0.01),
        'norm_final_g': gain((D_MODEL,)),
    }


def reference(x_prompt, x_sample, mem_prompt, cache_mla_ckv, cache_mla_krope, cache_mem_k, cache_mem_v,
              state_rg_h, state_rg_conv, norm_mix_g, w_in, b_in, rg_conv_w, rg_conv_b, rg_a_w, rg_a_b,
              rg_x_w, rg_x_b, rg_lambda, w_rnn_proj, mla_q_norm_g, mla_w_uq_nope, mla_w_uq_rope,
              mla_kv_norm_g, mla_w_uk, mla_w_uv, w_mla_proj, w_out, norm_x_g, norm_mem_g, w_xq, w_xk,
              w_xv, w_xo, norm_ffn_g, router_w, router_b, w_gate, b_gate, w_up, b_up, w_down, b_down,
              norm_final_g):
    bp = x_prompt.shape[0]
    pos_p = jnp.arange(x_prompt.shape[1], dtype=jnp.int32)
    pos_s = cache_mla_ckv.shape[2] + jnp.arange(x_sample.shape[1], dtype=jnp.int32)
    buf0 = jnp.zeros((bp, CONV_W - 1, D_RNN), x_prompt.dtype)
    h00 = jnp.zeros((bp, D_RNN), jnp.float32)
    xp, xs = x_prompt, x_sample
    hp, cp, ckp, krp, mkp, mvp = [], [], [], [], [], []
    hs, cs, cks, krs = [], [], [], []
    for l in range(DEPTH):
        p = {
            'norm_mix_g': norm_mix_g[l], 'w_in': w_in[l], 'b_in': b_in[l],
            'rg_conv_w': rg_conv_w[l], 'rg_conv_b': rg_conv_b[l], 'rg_a_w': rg_a_w[l], 'rg_a_b': rg_a_b[l],
            'rg_x_w': rg_x_w[l], 'rg_x_b': rg_x_b[l], 'rg_lambda': rg_lambda[l], 'w_rnn_proj': w_rnn_proj[l],
            'mla_q_norm_g': mla_q_norm_g[l], 'mla_w_uq_nope': mla_w_uq_nope[l], 'mla_w_uq_rope': mla_w_uq_rope[l],
            'mla_kv_norm_g': mla_kv_norm_g[l], 'mla_w_uk': mla_w_uk[l], 'mla_w_uv': mla_w_uv[l],
            'w_mla_proj': w_mla_proj[l], 'w_out': w_out[l], 'norm_x_g': norm_x_g[l],
            'w_xq': w_xq[l], 'w_xo': w_xo[l], 'norm_ffn_g': norm_ffn_g[l],
            'router_w': router_w[l], 'router_b': router_b[l], 'w_gate': w_gate[l], 'b_gate': b_gate[l],
            'w_up': w_up[l], 'b_up': b_up[l], 'w_down': w_down[l], 'b_down': b_down[l],
        }
        mk, mv = memory_kv(mem_prompt, norm_mem_g[l], w_xk[l], w_xv[l])
        xp, buf_p, hl_p, ckv_p, kr_p = layer(xp, pos_p, buf0, h00, None, None, mk, mv, p)
        xs, buf_s, hl_s, ckv_s, kr_s = layer(xs, pos_s, state_rg_conv[l], state_rg_h[l], cache_mla_ckv[l],
                                             cache_mla_krope[l], cache_mem_k[l], cache_mem_v[l], p)
        hp.append(hl_p.astype(state_rg_h.dtype)); cp.append(buf_p); ckp.append(ckv_p); krp.append(kr_p)
        mkp.append(mk); mvp.append(mv)
        hs.append(hl_s.astype(state_rg_h.dtype)); cs.append(buf_s); cks.append(ckv_s); krs.append(kr_s)
    y_prompt = rmsnorm(xp, norm_final_g)
    y_sample = rmsnorm(xs, norm_final_g)
    return (y_prompt, y_sample, jnp.stack(hp), jnp.stack(cp), jnp.stack(ckp), jnp.stack(krp),
            jnp.stack(mkp), jnp.stack(mvp), jnp.stack(hs), jnp.stack(cs), jnp.stack(cks), jnp.stack(krs))
```

```python
import functools

import jax
import jax.numpy as jnp
from jax import lax
from jax.experimental import pallas as pl
from jax.experimental.pallas import tpu as pltpu

F32, BF16, I32, U32 = jnp.float32, jnp.bfloat16, jnp.int32, jnp.uint32

CHUNK = 64
NORM_EPS = 1e-6
RG_C = 8.0
CONV_W = 4
ROPE_BASE = 10000.0
SWIGLU_LIMIT = 7.0
SWIGLU_ALPHA = 1.702
TOP_K = 4
LANES = 128
SUBLANES = 8
VMEM_LIMIT = 56 * 1024 * 1024
HI_MASK = 0xFFFF0000
NEG = -1e30


def _pick(n, prefs):
    for p in prefs:
        if n % p == 0:
            return p
    raise ValueError(f"no tile for {n} in {prefs}")


def _cparams(*sem):
    return pltpu.CompilerParams(dimension_semantics=sem, vmem_limit_bytes=VMEM_LIMIT)


def _rms(x, g):
    return x * lax.rsqrt(jnp.mean(x * x, axis=-1, keepdims=True) + NORM_EPS) * g


def _dot(a, b):
    return jnp.dot(a, b, preferred_element_type=F32)


def _dot_t(a, b):
    return lax.dot_general(a, b, (((1,), (1,)), ((), ())), preferred_element_type=F32)


def _pack(x, grp):
    bits = lax.bitcast_convert_type(x.astype(BF16).astype(F32), U32)
    half = grp // 2
    outs = []
    for g in range(x.shape[1] // grp):
        lo = bits[:, g * grp:g * grp + half]
        hi = bits[:, g * grp + half:(g + 1) * grp]
        outs.append((lo >> 16) | (hi & jnp.uint32(HI_MASK)))
    return outs[0] if len(outs) == 1 else jnp.concatenate(outs, axis=1)


def _unpack(p, grp):
    lo = lax.bitcast_convert_type(p << 16, F32)
    hi = lax.bitcast_convert_type(p & jnp.uint32(HI_MASK), F32)
    half = grp // 2
    outs = []
    for g in range(p.shape[1] // half):
        outs.append(lo[:, g * half:(g + 1) * half])
        outs.append(hi[:, g * half:(g + 1) * half])
    return jnp.concatenate(outs, axis=1)


def _rmsnorm_kernel(x_ref, g_ref, o_ref):
    o_ref[...] = _rms(x_ref[...].astype(F32), g_ref[...]).astype(o_ref.dtype)


def rmsnorm_rows(x, g, out_dtype):
    m, n = x.shape
    tm = _pick(m, (512, 256, 128, 64, 8))
    return pl.pallas_call(
        _rmsnorm_kernel,
        grid=(m // tm,),
        in_specs=[pl.BlockSpec((tm, n), lambda i: (i, 0)), pl.BlockSpec((1, n), lambda i: (0, 0))],
        out_specs=pl.BlockSpec((tm, n), lambda i: (i, 0)),
        out_shape=jax.ShapeDtypeStruct((m, n), out_dtype),
        compiler_params=_cparams("parallel"),
    )(x, g.reshape(1, n).astype(F32))


def _mm_kernel(*refs, has_bias, epilogue):
    x_ref, w_ref = refs[0], refs[1]
    acc = _dot(x_ref[...].astype(BF16), w_ref[...])
    k = 2
    if has_bias:
        acc = acc + refs[k][...]
        k += 1
    extras = [r[...] for r in refs[k:-1]]
    o_ref = refs[-1]
    o_ref[...] = epilogue(acc, *extras).astype(o_ref.dtype)


def matmul(x, w, bias=None, extras=(), epilogue=lambda a: a, out_dtype=F32, m_rows=None, tn=None):
    m = x.shape[0] if m_rows is None else m_rows
    kdim, n = w.shape
    tm = _pick(m, (1024, 512, 256, 128, 64, 8))
    tn = tn or _pick(n, (512, 256, 128))
    in_specs = [pl.BlockSpec((tm, kdim), lambda i, j: (i, 0)), pl.BlockSpec((kdim, tn), lambda i, j: (0, j))]
    args = [x, w]
    if bias is not None:
        in_specs.append(pl.BlockSpec((1, tn), lambda i, j: (0, j)))
        args.append(bias.reshape(1, n).astype(F32))
    for arr, kind in extras:
        if kind == "tile":
            in_specs.append(pl.BlockSpec((tm, tn), lambda i, j: (i, j)))
        elif kind == "row":
            in_specs.append(pl.BlockSpec((1, tn), lambda i, j: (0, j)))
        else:
            in_specs.append(pl.BlockSpec((tm, arr.shape[1]), lambda i, j: (i, 0)))
        args.append(arr)
    return pl.pallas_call(
        functools.partial(_mm_kernel, has_bias=bias is not None, epilogue=epilogue),
        grid=(m // tm, n // tn),
        in_specs=in_specs,
        out_specs=pl.BlockSpec((tm, tn), lambda i, j: (i, j)),
        out_shape=jax.ShapeDtypeStruct((m, n), out_dtype),
        compiler_params=_cparams("parallel", "parallel"),
    )(*args)


def _rglru_kernel(xr_ref, yg_ref, conv0_ref, h0_ref, cw_ref, cb_ref, wa_ref, ba_ref, wx_ref, bx_ref,
                  lam_ref, *rest, tt, nblk, n_alias):
    hy_ref, hl_ref, cx, ch = rest[n_alias:]
    t = pl.program_id(2)

    @pl.when(t == 0)
    def _():
        cx[...] = conv0_ref[0]
        ch[...] = h0_ref[0]

    xr = xr_ref[...]
    xp = jnp.concatenate([cx[...], xr], axis=0)
    cw = cw_ref[...]
    xc = cb_ref[...]
    for k in range(CONV_W):
        off = SUBLANES - (CONV_W - 1) + k
        xc = xc + xp[off:off + tt] * cw[k:k + 1]
    cx[...] = xr[tt - SUBLANES:]

    ra, ia = [], []
    for b in range(nblk):
        xb = xc[:, b * LANES:(b + 1) * LANES].astype(BF16)
        ra.append(_dot(xb, wa_ref[b]))
        ia.append(_dot(xb, wx_ref[b]))
    cat = (lambda v: v[0] if len(v) == 1 else jnp.concatenate(v, axis=1))
    r = jax.nn.sigmoid(cat(ra) + ba_ref[...])
    gi = jax.nn.sigmoid(cat(ia) + bx_ref[...])
    log_a = -RG_C * r * jax.nn.softplus(-lam_ref[...])
    a = jnp.exp(log_a)
    u = jnp.sqrt(-jnp.tanh(log_a) * (1.0 + a * a)) * (gi * xc)

    row = lax.broadcasted_iota(I32, (tt, 1), 0)
    d = 1
    while d < tt:
        keep = row >= d
        a_s = jnp.where(keep, pltpu.roll(a, d, 0), 1.0)
        u_s = jnp.where(keep, pltpu.roll(u, d, 0), 0.0)
        u = a * u_s + u
        a = a * a_s
        d *= 2
    h = u + a * ch[0:1]
    last = h[tt - 1:tt]
    ch[...] = jnp.broadcast_to(last, ch.shape)
    hy_ref[...] = (h * yg_ref[...].astype(F32)).astype(hy_ref.dtype)
    hl_ref[0] = jnp.broadcast_to(last, hl_ref.shape[1:])


def rglru(xr, yg, conv0, h0, cw, cb, wa, ba, wx, bx, lam, *, nseq, seqlen, row0, hy_prev=None):
    t_all, d = xr.shape
    tt = _pick(seqlen, (256, 128, 64))
    nt = seqlen // tt
    dc = _pick(d, (512, 256, 128))
    nblk = dc // LANES
    rb0 = row0 // tt
    rowmap = lambda s, c, t: (rb0 + s * nt + t, c)
    vec = lambda s, c, t: (0, c)
    in_specs = [
        pl.BlockSpec((tt, dc), rowmap), pl.BlockSpec((tt, dc), rowmap),
        pl.BlockSpec((1, SUBLANES, dc), lambda s, c, t: (s, 0, c)),
        pl.BlockSpec((1, SUBLANES, dc), lambda s, c, t: (s, 0, c)),
        pl.BlockSpec((SUBLANES, dc), vec), pl.BlockSpec((1, dc), vec),
        pl.BlockSpec((nblk, LANES, LANES), lambda s, c, t: (c, 0, 0)), pl.BlockSpec((1, dc), vec),
        pl.BlockSpec((nblk, LANES, LANES), lambda s, c, t: (c, 0, 0)), pl.BlockSpec((1, dc), vec),
        pl.BlockSpec((1, dc), vec),
    ]
    args = [xr, yg, conv0, h0, cw, cb, wa, ba, wx, bx, lam]
    aliases = {}
    if hy_prev is not None:
        in_specs.append(pl.BlockSpec(memory_space=pl.ANY))
        args.append(hy_prev)
        aliases = {len(args) - 1: 0}
    return pl.pallas_call(
        functools.partial(_rglru_kernel, tt=tt, nblk=nblk, n_alias=len(aliases)),
        grid=(nseq, d // dc, nt),
        in_specs=in_specs,
        out_specs=[pl.BlockSpec((tt, dc), rowmap), pl.BlockSpec((1, SUBLANES, dc), lambda s, c, t: (s, 0, c))],
        out_shape=[jax.ShapeDtypeStruct((t_all, d), BF16), jax.ShapeDtypeStruct((nseq, SUBLANES, d), F32)],
        scratch_shapes=[pltpu.VMEM((SUBLANES, dc), F32), pltpu.VMEM((SUBLANES, dc), F32)],
        input_output_aliases=aliases,
        compiler_params=_cparams("parallel", "parallel", "arbitrary"),
    )(*args)


def _attn_prompt_kernel(q_ref, k_ref, v_ref, kr_ref, o_ref, *, tq):
    qi = pl.program_id(2)
    q = q_ref[...]

    def scores(j0):
        k = jnp.concatenate([k_ref[pl.ds(j0, tq), :], kr_ref[pl.ds(j0, tq), :].astype(BF16)], axis=1)
        return _dot_t(q, k)

    def update(carry, s, j0):
        m, l, acc = carry
        m_new = jnp.maximum(m, jnp.max(s, axis=1, keepdims=True))
        alpha = jnp.exp(m - m_new)
        p = jnp.exp(s - m_new)
        l = alpha * l + jnp.sum(p, axis=1, keepdims=True)
        acc = alpha * acc + _dot(p.astype(BF16), v_ref[pl.ds(j0, tq), :])
        return m_new, l, acc

    def body(j, carry):
        j0 = pl.multiple_of(j * tq, tq)
        return update(carry, scores(j0), j0)

    init = (jnp.full((tq, 1), NEG, F32), jnp.zeros((tq, 1), F32), jnp.zeros((tq, LANES), F32))
    carry = lax.fori_loop(0, qi, body, init)
    j0 = pl.multiple_of(qi * tq, tq)
    qc = lax.broadcasted_iota(I32, (tq, tq), 0) // CHUNK
    kc = lax.broadcasted_iota(I32, (tq, tq), 1) // CHUNK
    s = jnp.where(kc <= qc, scores(j0), NEG)
    _, l, acc = update(carry, s, j0)
    o_ref[...] = (acc / l).astype(o_ref.dtype)


def attn_prompt(q, kv, krr, *, nbatch, seqlen, nheads):
    t_all = q.shape[0]
    tq = _pick(seqlen, (512, 256, 128, 64))
    nq = seqlen // tq
    return pl.pallas_call(
        functools.partial(_attn_prompt_kernel, tq=tq),
        grid=(nbatch, nheads, nq),
        in_specs=[
            pl.BlockSpec((tq, 2 * LANES), lambda b, h, i: (b * nq + i, h)),
            pl.BlockSpec((seqlen, LANES), lambda b, h, i: (b, h)),
            pl.BlockSpec((seqlen, LANES), lambda b, h, i: (b, nheads + h)),
            pl.BlockSpec((seqlen, LANES), lambda b, h, i: (b, 0)),
        ],
        out_specs=pl.BlockSpec((tq, LANES), lambda b, h, i: (b * nq + i, h)),
        out_shape=jax.ShapeDtypeStruct((t_all, nheads * LANES), BF16),
        compiler_params=_cparams("parallel", "parallel", "arbitrary"),
    )(q, kv, kv, krr)


def _attn_sample_kernel(q_ref, cn_ref, kr_ref, past_ref, pkr_ref, wukt_ref, wuv_ref, o_prev, o_ref, *, nheads):
    del o_prev
    q = q_ref[...]
    s_len = q.shape[0]
    lat = jnp.concatenate([past_ref[0].astype(BF16), cn_ref[...].astype(BF16)], axis=0)
    kro = jnp.concatenate([pkr_ref[0].astype(BF16), kr_ref[...].astype(BF16)], axis=0)
    kext = jnp.concatenate([lat, kro], axis=1)
    qs = []
    for h in range(nheads):
        qn = q[:, h * 2 * LANES:h * 2 * LANES + LANES]
        qr = q[:, h * 2 * LANES + LANES:(h + 1) * 2 * LANES]
        qs.append(jnp.concatenate([_dot(qn, wukt_ref[h]).astype(BF16), qr], axis=1))
    qext = jnp.concatenate(qs, axis=0)
    s = _dot_t(qext, kext)
    m = jnp.max(s, axis=1, keepdims=True)
    p = jnp.exp(s - m)
    l = jnp.sum(p, axis=1, keepdims=True)
    ol = (_dot(p.astype(BF16), lat) / l).astype(BF16)
    outs = [_dot(ol[h * s_len:(h + 1) * s_len], wuv_ref[h]) for h in range(nheads)]
    o_ref[...] = jnp.concatenate(outs, axis=1).astype(o_ref.dtype)


def attn_sample(q, ckv, krr, past_ckv, past_kr2, wukt, wuv, o_prev, *, nbatch, seqlen, row0, nheads):
    t_all = q.shape[0]
    rb0 = row0 // seqlen
    p_len, c = past_ckv.shape[1:]
    return pl.pallas_call(
        functools.partial(_attn_sample_kernel, nheads=nheads),
        grid=(nbatch,),
        in_specs=[
            pl.BlockSpec((seqlen, nheads * 2 * LANES), lambda b: (rb0 + b, 0)),
            pl.BlockSpec((seqlen, c), lambda b: (rb0 + b, 0)),
            pl.BlockSpec((seqlen, LANES), lambda b: (rb0 + b, 0)),
            pl.BlockSpec((1, p_len, c), lambda b: (b, 0, 0)),
            pl.BlockSpec((1, p_len, LANES), lambda b: (b, 0, 0)),
            pl.BlockSpec(wukt.shape, lambda b: (0, 0, 0)),
            pl.BlockSpec(wuv.shape, lambda b: (0, 0, 0)),
            pl.BlockSpec(memory_space=pl.ANY),
        ],
        out_specs=pl.BlockSpec((seqlen, nheads * LANES), lambda b: (rb0 + b, 0)),
        out_shape=jax.ShapeDtypeStruct((t_all, nheads * LANES), BF16),
        input_output_aliases={7: 0},
        compiler_params=_cparams("parallel"),
    )(q, ckv, krr, past_ckv, past_kr2, wukt, wuv, o_prev)


def _xattn_kernel(x1_ref, gx_ref, wq_ref, mk_ref, mv_ref, wo_ref, gf_ref, rwh_ref, rwl_ref, rb_ref, cnt0_ref,
                  *rest, nb, seg, xheads, n_alias):
    x2_ref, hp_ref, ir_ref, gw_ref, cnt_ref, carry = rest[n_alias:]
    i = pl.program_id(0)

    @pl.when(i == 0)
    def _():
        carry[...] = cnt0_ref[...]

    x1 = x1_ref[...]
    tm = x1.shape[0]
    hn = _rms(x1, gx_ref[...]).astype(BF16)
    q = _dot(hn, wq_ref[...]).astype(BF16)
    scale = LANES ** -0.5
    segs = []
    for n in range(nb):
        heads = []
        for h in range(xheads):
            cols = slice(h * LANES, (h + 1) * LANES)
            qh = q[n * seg:(n + 1) * seg, cols]
            kh = mk_ref[n][:, cols].astype(BF16)
            vh = mv_ref[n][:, cols].astype(BF16)
            s = _dot_t(qh, kh) * scale
            m = jnp.max(s, axis=1, keepdims=True)
            p = jnp.exp(s - m)
            l = jnp.sum(p, axis=1, keepdims=True)
            heads.append(_dot(p.astype(BF16), vh) / l)
        segs.append(jnp.concatenate(heads, axis=1))
    o = (segs[0] if nb == 1 else jnp.concatenate(segs, axis=0)).astype(BF16)
    x2 = x1 + _dot(o, wo_ref[...])
    x2_ref[...] = x2

    hf = _rms(x2, gf_ref[...])
    hp_ref[...] = _pack(hf, hf.shape[1])
    hb = hf.astype(BF16)
    hl = (hf - hb.astype(F32)).astype(BF16)
    logits = _dot(hb, rwh_ref[...]) + _dot(hl, rwh_ref[...]) + _dot(hb, rwl_ref[...]) + rb_ref[...]

    lane = lax.broadcasted_iota(I32, (tm, LANES), 1)
    work = logits
    sels, vals, idxs = [], [], []
    for _ in range(TOP_K):
        m = jnp.max(work, axis=1, keepdims=True)
        idx = jnp.min(jnp.where(work == m, lane, LANES), axis=1, keepdims=True)
        sel = lane == idx
        sels.append(sel)
        vals.append(m)
        idxs.append(idx)
        work = jnp.where(sel, -jnp.inf, work)
    es = [jnp.exp(v - vals[0]) for v in vals]
    den = es[0] + es[1] + es[2] + es[3]
    onehot = (sels[0] | sels[1] | sels[2] | sels[3]).astype(F32)
    tri = (lax.broadcasted_iota(I32, (tm, tm), 0) > lax.broadcasted_iota(I32, (tm, tm), 1)).astype(BF16)
    excl = _dot(tri, onehot.astype(BF16)) + carry[0:1]
    carry[...] = carry[...] + jnp.sum(onehot, axis=0, keepdims=True)
    ir = jnp.zeros((tm, LANES), I32)
    gw = jnp.zeros((tm, LANES), F32)
    for k in range(TOP_K):
        rank = jnp.sum(jnp.where(sels[k], excl, 0.0), axis=1, keepdims=True).astype(I32)
        ir = jnp.where(lane == k, idxs[k], ir)
        ir = jnp.where(lane == TOP_K + k, rank, ir)
        gw = jnp.where(lane == k, es[k] / den, gw)
    ir_ref[...] = ir
    gw_ref[...] = gw
    cnt_ref[...] = carry[...]


def xattn_router(x1, gx, wq, mk, mv, wo, gf, rwh, rwl, rb, cnt0, *, nb, seg, ntiles, row0, xheads, prev=None):
    t_all, d = x1.shape
    tm = nb * seg
    rb0 = row0 // tm
    rows = lambda i: (rb0 + i, 0)
    const = lambda i: (0, 0)
    mem_rows, mem_cols = mk.shape[1:]
    if nb == 1:
        tiles_per_batch = ntiles // mk.shape[0]
        mem_map = lambda i: (i // tiles_per_batch, 0, 0)
    else:
        mem_map = lambda i: (i, 0, 0)
    in_specs = [
        pl.BlockSpec((tm, d), rows), pl.BlockSpec((1, d), const), pl.BlockSpec(wq.shape, const),
        pl.BlockSpec((nb, mem_rows, mem_cols), mem_map), pl.BlockSpec((nb, mem_rows, mem_cols), mem_map),
        pl.BlockSpec(wo.shape, const), pl.BlockSpec((1, d), const),
        pl.BlockSpec(rwh.shape, const), pl.BlockSpec(rwl.shape, const), pl.BlockSpec((1, LANES), const),
        pl.BlockSpec((SUBLANES, LANES), const),
    ]
    args = [x1, gx, wq, mk, mv, wo, gf, rwh, rwl, rb, cnt0]
    aliases = {}
    if prev is not None:
        for k, arr in enumerate(prev):
            in_specs.append(pl.BlockSpec(memory_space=pl.ANY))
            args.append(arr)
            aliases[len(args) - 1] = k
    out_shape = [
        jax.ShapeDtypeStruct((t_all, d), F32), jax.ShapeDtypeStruct((t_all, d // 2), U32),
        jax.ShapeDtypeStruct((t_all, LANES), I32), jax.ShapeDtypeStruct((t_all, LANES), F32),
        jax.ShapeDtypeStruct((SUBLANES, LANES), F32),
    ]
    out_specs = [
        pl.BlockSpec((tm, d), rows), pl.BlockSpec((tm, d // 2), rows),
        pl.BlockSpec((tm, LANES), rows), pl.BlockSpec((tm, LANES), rows),
        pl.BlockSpec((SUBLANES, LANES), const),
    ]
    return pl.pallas_call(
        functools.partial(_xattn_kernel, nb=nb, seg=seg, xheads=xheads, n_alias=len(aliases)),
        grid=(ntiles,),
        in_specs=in_specs,
        out_specs=out_specs,
        out_shape=out_shape,
        scratch_shapes=[pltpu.VMEM((SUBLANES, LANES), F32)],
        input_output_aliases=aliases,
        compiler_params=_cparams("arbitrary"),
    )(*args)


def _row_copy(src, dst, i, j, sem):
    return pltpu.make_async_copy(src.at[pl.ds(i, 1)], dst.at[pl.ds(j, 1)], sem)


def _dispatch_kernel(offs_ref, ir_ref, h_hbm, xb_in, xb_out, sem, *, tmd):
    del xb_in
    base = pl.program_id(0) * tmd

    def issue(t, c):
        for k in range(TOP_K):
            dst = offs_ref[ir_ref[k, t]] + ir_ref[TOP_K + k, t]
            _row_copy(h_hbm, xb_out, base + t, dst, sem.at[0]).start()
        return c

    lax.fori_loop(0, tmd, issue, 0)

    def drain(t, c):
        _row_copy(h_hbm, xb_out, 0, 0, sem.at[0]).wait()
        return c

    lax.fori_loop(0, tmd * TOP_K, drain, 0)


def moe_dispatch(offs, ir_t, hp, n_rows):
    t_all, w = hp.shape
    tmd = _pick(t_all, (512, 256, 128))
    xb0 = jnp.zeros((n_rows, w), U32)
    return pl.pallas_call(
        functools.partial(_dispatch_kernel, tmd=tmd),
        grid_spec=pltpu.PrefetchScalarGridSpec(
            num_scalar_prefetch=1,
            grid=(t_all // tmd,),
            in_specs=[
                pl.BlockSpec((2 * TOP_K, tmd), lambda i, offs: (0, i), memory_space=pltpu.SMEM),
                pl.BlockSpec(memory_space=pl.ANY),
                pl.BlockSpec(memory_space=pl.ANY),
            ],
            out_specs=pl.BlockSpec(memory_space=pl.ANY),
            scratch_shapes=[pltpu.SemaphoreType.DMA((1,))],
        ),
        out_shape=jax.ShapeDtypeStruct((n_rows, w), U32),
        input_output_aliases={3: 0},
        compiler_params=_cparams("arbitrary"),
    )(offs, ir_t, hp, xb0)


def _expert_up_kernel(se, sj, si, sf, sv, xb_ref, wg_ref, wu_ref, bg_ref, bu_ref, o_ref, wgb, wub):
    s = pl.program_id(0)

    @pl.when(sf[s] == 1)
    def _():
        wgb[...] = wg_ref[0].astype(BF16)
        wub[...] = wu_ref[0].astype(BF16)

    @pl.when(sv[s] == 1)
    def _():
        p = xb_ref[...]
        x = _unpack(p, 2 * p.shape[1]).astype(BF16)
        g = jnp.minimum(_dot(x, wgb[...]) + bg_ref[0], SWIGLU_LIMIT)
        u = jnp.clip(_dot(x, wub[...]) + bu_ref[0], -SWIGLU_LIMIT, SWIGLU_LIMIT)
        o_ref[...] = ((u + 1.0) * g * jax.nn.sigmoid(SWIGLU_ALPHA * g)).astype(o_ref.dtype)


def _expert_down_kernel(se, sj, si, sf, sv, a_ref, wd_ref, bd_ref, o_ref, wdb):
    s = pl.program_id(0)

    @pl.when(sf[s] == 1)
    def _():
        wdb[...] = wd_ref[0].astype(BF16)

    @pl.when(sv[s] == 1)
    def _():
        y = _dot(a_ref[...], wdb[...]) + bd_ref[0]
        o_ref[...] = _pack(y, y.shape[1])


def _expert_schedule(nblk_e, nj, nb_max):
    cum = jnp.cumsum(nblk_e)
    start = cum - nblk_e
    total = cum[-1]
    ns = nj * nb_max
    s = jnp.minimum(jnp.arange(ns, dtype=I32), nj * total - 1)
    e = jnp.searchsorted(nj * cum, s, side="right").astype(I32)
    local = s - nj * start[e]
    n_e = jnp.maximum(nblk_e[e], 1)
    j = local // n_e
    r = local % n_e
    valid = (jnp.arange(ns, dtype=I32) < nj * total).astype(I32)
    first = ((r == 0).astype(I32)) * valid
    return e, j.astype(I32), (start[e] + r).astype(I32), first, valid


def moe_experts(xb, nblk_e, w_gate, b_gate, w_up, b_up, w_down, b_down, *, tme):
    n_rows, half = xb.shape
    d = 2 * half
    ne, _, dff = w_gate.shape
    nb_max = n_rows // tme
    tn_up = _pick(dff, (512, 256, 128))
    nj = dff // tn_up
    sched = _expert_schedule(nblk_e, nj, nb_max)
    act = pl.pallas_call(
        _expert_up_kernel,
        grid_spec=pltpu.PrefetchScalarGridSpec(
            num_scalar_prefetch=5,
            grid=(nj * nb_max,),
            in_specs=[
                pl.BlockSpec((tme, half), lambda s, se, sj, si, sf, sv: (si[s], 0)),
                pl.BlockSpec((1, d, tn_up), lambda s, se, sj, si, sf, sv: (se[s], 0, sj[s])),
                pl.BlockSpec((1, d, tn_up), lambda s, se, sj, si, sf, sv: (se[s], 0, sj[s])),
                pl.BlockSpec((1, 1, tn_up), lambda s, se, sj, si, sf, sv: (se[s], 0, sj[s])),
                pl.BlockSpec((1, 1, tn_up), lambda s, se, sj, si, sf, sv: (se[s], 0, sj[s])),
            ],
            out_specs=pl.BlockSpec((tme, tn_up), lambda s, se, sj, si, sf, sv: (si[s], sj[s])),
            scratch_shapes=[pltpu.VMEM((d, tn_up), BF16), pltpu.VMEM((d, tn_up), BF16)],
        ),
        out_shape=jax.ShapeDtypeStruct((n_rows, dff), BF16),
        compiler_params=_cparams("arbitrary"),
    )(*sched, xb, w_gate, w_up, b_gate.reshape(ne, 1, dff), b_up.reshape(ne, 1, dff))

    tn_dn = _pick(d, (1024, 512, 256))
    nj2 = d // tn_dn
    sched2 = _expert_schedule(nblk_e, nj2, nb_max)
    yb = pl.pallas_call(
        _expert_down_kernel,
        grid_spec=pltpu.PrefetchScalarGridSpec(
            num_scalar_prefetch=5,
            grid=(nj2 * nb_max,),
            in_specs=[
                pl.BlockSpec((tme, dff), lambda s, se, sj, si, sf, sv: (si[s], 0)),
                pl.BlockSpec((1, dff, tn_dn), lambda s, se, sj, si, sf, sv: (se[s], 0, sj[s])),
                pl.BlockSpec((1, 1, tn_dn), lambda s, se, sj, si, sf, sv: (se[s], 0, sj[s])),
            ],
            out_specs=pl.BlockSpec((tme, tn_dn // 2), lambda s, se, sj, si, sf, sv: (si[s], sj[s])),
            scratch_shapes=[pltpu.VMEM((dff, tn_dn), BF16)],
        ),
        out_shape=jax.ShapeDtypeStruct((n_rows, d // 2), U32),
        compiler_params=_cparams("arbitrary"),
    )(*sched2, act, w_down, b_down.reshape(ne, 1, d))
    return yb, tn_dn


def _combine_kernel(offs_ref, ir_ref, gw_ref, x2_ref, g_ref, yb_hbm, o_ref, buf, sem, *, tmc, grp):
    def issue(t, c):
        for k in range(TOP_K):
            src = offs_ref[ir_ref[k, t]] + ir_ref[TOP_K + k, t]
            _row_copy(yb_hbm, buf.at[k], src, t, sem.at[0]).start()
        return c

    lax.fori_loop(0, tmc, issue, 0)

    def drain(t, c):
        _row_copy(yb_hbm, buf.at[0], 0, 0, sem.at[0]).wait()
        return c

    lax.fori_loop(0, tmc * TOP_K, drain, 0)
    y = x2_ref[...]
    gw = gw_ref[...]
    for k in range(TOP_K):
        y = y + gw[:, k:k + 1] * _unpack(buf[k], grp)
    o_ref[...] = _rms(y, g_ref[...])


def moe_combine(offs, ir_t, gw, x2, g, yb, *, rows, row0, grp):
    d = x2.shape[1]
    tmc = _pick(rows, (256, 128))
    rb0 = row0 // tmc
    return pl.pallas_call(
        functools.partial(_combine_kernel, tmc=tmc, grp=grp),
        grid_spec=pltpu.PrefetchScalarGridSpec(
            num_scalar_prefetch=1,
            grid=(rows // tmc,),
            in_specs=[
                pl.BlockSpec((2 * TOP_K, tmc), lambda i, offs: (0, rb0 + i), memory_space=pltpu.SMEM),
                pl.BlockSpec((tmc, LANES), lambda i, offs: (rb0 + i, 0)),
                pl.BlockSpec((tmc, d), lambda i, offs: (rb0 + i, 0)),
                pl.BlockSpec((1, d), lambda i, offs: (0, 0)),
                pl.BlockSpec(memory_space=pl.ANY),
            ],
            out_specs=pl.BlockSpec((tmc, d), lambda i, offs: (i, 0)),
            scratch_shapes=[pltpu.VMEM((TOP_K, tmc, d // 2), U32), pltpu.SemaphoreType.DMA((1,))],
        ),
        out_shape=jax.ShapeDtypeStruct((rows, d), F32),
        compiler_params=_cparams("arbitrary"),
    )(offs, ir_t, gw, x2, g.reshape(1, d).astype(F32), yb)


def _rot_cols(w):
    half = w.shape[-1] // 2
    return jnp.concatenate([-w[..., half:], w[..., :half]], axis=-1)


def kernel(x_prompt, x_sample, mem_prompt, cache_mla_ckv, cache_mla_krope, cache_mem_k, cache_mem_v, state_rg_h, state_rg_conv, norm_mix_g, w_in, b_in, rg_conv_w, rg_conv_b, rg_a_w, rg_a_b, rg_x_w, rg_x_b, rg_lambda, w_rnn_proj, mla_q_norm_g, mla_w_uq_nope, mla_w_uq_rope, mla_kv_norm_g, mla_w_uk, mla_w_uv, w_mla_proj, w_out, norm_x_g, norm_mem_g, w_xq, w_xk, w_xv, w_xo, norm_ffn_g, router_w, router_b, w_gate, b_gate, w_up, b_up, w_down, b_down, norm_final_g):
    nb_p, s_p, d = x_prompt.shape
    nb_s, s_s, _ = x_sample.shape
    assert w_in.shape[0] == 1, "single-layer step"
    t_p, t_s = nb_p * s_p, nb_s * s_s
    t_all = t_p + t_s
    d_rnn = rg_lambda.shape[-1]
    q_lora, kv_lora = mla_q_norm_g.shape[-1], mla_kv_norm_g.shape[-1]
    nheads, qk_nope = mla_w_uk.shape[2:]
    rope_dim = cache_mla_krope.shape[-1]
    v_dim = mla_w_uv.shape[-1]
    p_len = cache_mla_ckv.shape[2]
    n_mem, xheads, xhd = cache_mem_k.shape[2:]
    ne = router_w.shape[-1]
    assert qk_nope == LANES and v_dim == LANES and xhd == LANES and 2 * rope_dim == LANES
    assert p_len % CHUNK == 0 and s_s <= CHUNK and ne <= LANES

    wi, bi = w_in[0], b_in[0]
    o1, o2 = d_rnn, 2 * d_rnn
    o3, o4 = o2 + q_lora, o2 + q_lora + kv_lora
    o5 = o4 + rope_dim
    o6 = o5 + d
    seg = lambda a, b: (wi[:, a:b].astype(BF16), bi[a:b])
    w_xr, b_xr = seg(0, o1)
    w_yr, b_yr = seg(o1, o2)
    w_cq, b_cq = seg(o2, o3)
    w_ckv, b_ckv = seg(o3, o4)
    w_gr, b_gr = seg(o5, o6)
    w_gm, b_gm = seg(o6, o6 + d)
    w_kr = jnp.concatenate([wi[:, o4:o5], _rot_cols(wi[:, o4:o5])], axis=1).astype(BF16)
    b_kr = jnp.concatenate([bi[o4:o5], _rot_cols(bi[o4:o5])])
    w_q = jnp.concatenate([mla_w_uq_nope[0], mla_w_uq_rope[0], _rot_cols(mla_w_uq_rope[0])], axis=-1)
    w_q = w_q.reshape(q_lora, nheads * 2 * LANES).astype(BF16)
    w_kv = jnp.concatenate([mla_w_uk[0].reshape(kv_lora, -1), mla_w_uv[0].reshape(kv_lora, -1)], axis=1).astype(BF16)
    w_ukt = jnp.transpose(mla_w_uk[0], (1, 2, 0)).astype(BF16)
    w_uvh = jnp.transpose(mla_w_uv[0], (1, 0, 2)).astype(BF16)

    half = rope_dim // 2
    freq = ROPE_BASE ** (-jnp.arange(half, dtype=F32) / half)
    pos = jnp.concatenate([jnp.tile(jnp.arange(s_p, dtype=I32), nb_p),
                           jnp.tile(p_len + jnp.arange(s_s, dtype=I32), nb_s)]).astype(F32)
    ang = pos[:, None] * freq[None, :]
    cs, sn = jnp.cos(ang), jnp.sin(ang)
    ktab = jnp.concatenate([cs, cs, sn, sn], axis=1)
    mla_scale = float(qk_nope + rope_dim) ** -0.5
    qtab = jnp.concatenate([jnp.ones((t_all, LANES), F32), ktab], axis=1) * mla_scale

    x_all = jnp.concatenate([x_prompt.reshape(t_p, d), x_sample.reshape(t_s, d)], axis=0)
    hn = rmsnorm_rows(x_all, norm_mix_g[0], BF16)
    xr = matmul(hn, w_xr, b_xr)
    yg = matmul(hn, w_yr, b_yr, epilogue=jax.nn.gelu, out_dtype=BF16)
    sg_rnn = matmul(hn, w_gr, b_gr, epilogue=jax.nn.sigmoid, out_dtype=BF16)
    sg_mla = matmul(hn, w_gm, b_gm, epilogue=jax.nn.sigmoid, out_dtype=BF16)
    cqn = matmul(hn, w_cq, b_cq, extras=[(mla_q_norm_g[0].reshape(1, -1), "row")], epilogue=_rms,
                 out_dtype=BF16, tn=q_lora)
    ckv = matmul(hn, w_ckv, b_ckv, extras=[(mla_kv_norm_g[0].reshape(1, -1), "row")], epilogue=_rms, tn=kv_lora)

    def rope_epilogue(acc, tab):
        z = acc * tab
        return z + pltpu.roll(z, rope_dim, 1)

    krr = matmul(hn, w_kr, b_kr, extras=[(ktab, "col")], epilogue=rope_epilogue, tn=LANES)

    cw = jnp.pad(rg_conv_w[0], ((0, SUBLANES - CONV_W), (0, 0)))
    cb = rg_conv_b[0].reshape(1, d_rnn)
    wa, wx = rg_a_w[0].astype(BF16), rg_x_w[0].astype(BF16)
    ba, bx = rg_a_b[0].reshape(1, d_rnn), rg_x_b[0].reshape(1, d_rnn)
    lam = rg_lambda[0].reshape(1, d_rnn)
    zeros_state = jnp.zeros((nb_p, SUBLANES, d_rnn), F32)
    hy, hl_p = rglru(xr, yg, zeros_state, zeros_state, cw, cb, wa, ba, wx, bx, lam,
                     nseq=nb_p, seqlen=s_p, row0=0)
    conv_s = jnp.pad(state_rg_conv[0], ((0, 0), (SUBLANES - (CONV_W - 1), 0), (0, 0)))
    h0_s = jnp.broadcast_to(state_rg_h[0][:, None, :], (nb_s, SUBLANES, d_rnn)).astype(F32)
    hy, hl_s = rglru(xr, yg, conv_s, h0_s, cw, cb, wa, ba, wx, bx, lam,
                     nseq=nb_s, seqlen=s_s, row0=t_p, hy_prev=hy)
    m1 = matmul(hy, w_rnn_proj[0].astype(BF16), extras=[(sg_rnn, "tile")],
                epilogue=lambda a, g: a * g.astype(F32), out_dtype=BF16)

    def q_epilogue(acc, tab):
        reps = acc.shape[1] // tab.shape[1]
        return acc * (tab if reps == 1 else jnp.concatenate([tab] * reps, axis=1))

    q = matmul(cqn, w_q, extras=[(qtab, "col")], epilogue=q_epilogue, out_dtype=BF16,
               tn=_pick(nheads * 2 * LANES, (1024, 512, 256)))
    kvp = matmul(ckv, w_kv, out_dtype=BF16, m_rows=t_p)
    o = attn_prompt(q, kvp, krr, nbatch=nb_p, seqlen=s_p, nheads=nheads)
    past_kr2 = jnp.concatenate([cache_mla_krope[0], cache_mla_krope[0]], axis=-1)
    o = attn_sample(q, ckv, krr, cache_mla_ckv[0], past_kr2, w_ukt, w_uvh, o,
                    nbatch=nb_s, seqlen=s_s, row0=t_p, nheads=nheads)
    merged = matmul(o, w_mla_proj[0].astype(BF16), extras=[(sg_mla, "tile"), (m1, "tile")],
                    epilogue=lambda a, g, r: a * g.astype(F32) + r.astype(F32), out_dtype=BF16)
    x1 = matmul(merged, w_out[0].astype(BF16), extras=[(x_all, "tile")], epilogue=lambda a, r: a + r)

    xcols = xheads * xhd
    mn = rmsnorm_rows(mem_prompt.reshape(nb_p * n_mem, d), norm_mem_g[0], BF16)
    mk = matmul(mn, w_xk[0].reshape(d, xcols).astype(BF16))
    mv = matmul(mn, w_xv[0].reshape(d, xcols).astype(BF16))
    rw = jnp.pad(router_w[0], ((0, 0), (0, LANES - ne)))
    rwh = rw.astype(BF16)
    rwl = (rw - rwh.astype(F32)).astype(BF16)
    rb = jnp.pad(router_b[0].astype(F32), (0, LANES - ne), constant_values=NEG).reshape(1, LANES)
    wq_x = w_xq[0].reshape(d, xcols).astype(BF16)
    wo_x = w_xo[0].reshape(xcols, d).astype(BF16)
    gx, gf = norm_x_g[0].reshape(1, d), norm_ffn_g[0].reshape(1, d)
    tile_p = _pick(s_p, (512, 256, 128, 64))
    outs = xattn_router(x1, gx, wq_x, mk.reshape(nb_p, n_mem, xcols), mv.reshape(nb_p, n_mem, xcols), wo_x, gf,
                        rwh, rwl, rb, jnp.zeros((SUBLANES, LANES), F32),
                        nb=1, seg=tile_p, ntiles=t_p // tile_p, row0=0, xheads=xheads)
    nb_x = _pick(nb_s, (8, 4, 2, 1))
    outs = xattn_router(x1, gx, wq_x, cache_mem_k[0].reshape(nb_s, n_mem, xcols),
                        cache_mem_v[0].reshape(nb_s, n_mem, xcols), wo_x, gf, rwh, rwl, rb, outs[4],
                        nb=nb_x, seg=s_s, ntiles=nb_s // nb_x, row0=t_p, xheads=xheads, prev=outs[:4])
    x2, hp, ir, gw, cnt = outs

    tme = 256
    counts = cnt[0, :ne].astype(I32)
    nblk_e = (counts + tme - 1) // tme
    offs = ((jnp.cumsum(nblk_e) - nblk_e) * tme).astype(I32)
    n_rows = (t_all * TOP_K // tme + ne) * tme
    ir_t = jnp.transpose(ir[:, :2 * TOP_K])
    xb = moe_dispatch(offs, ir_t, hp, n_rows)
    yb, grp = moe_experts(xb, nblk_e, w_gate[0], b_gate[0], w_up[0], b_up[0], w_down[0], b_down[0], tme=tme)
    y_p = moe_combine(offs, ir_t, gw, x2, norm_final_g, yb, rows=t_p, row0=0, grp=grp)
    y_s = moe_combine(offs, ir_t, gw, x2, norm_final_g, yb, rows=t_s, row0=t_p, grp=grp)

    nconv = CONV_W - 1
    xr_p = xr[:t_p].reshape(nb_p, s_p, d_rnn)
    xr_s = xr[t_p:].reshape(nb_s, s_s, d_rnn)
    conv_p = xr_p[:, s_p - nconv:]
    conv_new_s = jnp.concatenate([state_rg_conv[0], xr_s], axis=1)[:, -nconv:]
    return (
        y_p.reshape(nb_p, s_p, d), y_s.reshape(nb_s, s_s, d),
        hl_p[None, :, 0, :], conv_p[None],
        ckv[:t_p].reshape(1, nb_p, s_p, kv_lora), krr[:t_p, :rope_dim].reshape(1, nb_p, s_p, rope_dim),
        mk.reshape(1, nb_p, n_mem, xheads, xhd), mv.reshape(1, nb_p, n_mem, xheads, xhd),
        hl_s[None, :, 0, :], conv_new_s[None],
        ckv[t_p:].reshape(1, nb_s, s_s, kv_lora), krr[t_p:, :rope_dim].reshape(1, nb_s, s_s, rope_dim),
    )
```

```python
import functools

import jax
import jax.numpy as jnp
from jax import lax
from jax.experimental import pallas as pl
from jax.experimental.pallas import tpu as pltpu

F32, BF16, I32, U32 = jnp.float32, jnp.bfloat16, jnp.int32, jnp.uint32

CHUNK = 64
NORM_EPS = 1e-6
RG_C = 8.0
CONV_W = 4
ROPE_BASE = 10000.0
SWIGLU_LIMIT = 7.0
SWIGLU_ALPHA = 1.702
TOP_K = 4
LANES = 128
SUBLANES = 8
VMEM_LIMIT = 56 * 1024 * 1024
DMA_UNROLL = 8
HI_MASK = 0xFFFF0000
NEG = -1e30


def _pick(n, prefs):
    for p in prefs:
        if n % p == 0:
            return p
    raise ValueError(f"no tile for {n} in {prefs}")


def _cparams(*sem):
    return pltpu.CompilerParams(dimension_semantics=sem, vmem_limit_bytes=VMEM_LIMIT)


def _rms(x, g):
    return x * lax.rsqrt(jnp.mean(x * x, axis=-1, keepdims=True) + NORM_EPS) * g


def _dot(a, b):
    return jnp.dot(a, b, preferred_element_type=F32)


def _dot_t(a, b):
    return lax.dot_general(a, b, (((1,), (1,)), ((), ())), preferred_element_type=F32)


def _pack(x, grp):
    bits = lax.bitcast_convert_type(x.astype(BF16).astype(F32), U32)
    half = grp // 2
    outs = []
    for g in range(x.shape[1] // grp):
        lo = bits[:, g * grp:g * grp + half]
        hi = bits[:, g * grp + half:(g + 1) * grp]
        outs.append((lo >> 16) | (hi & jnp.uint32(HI_MASK)))
    return outs[0] if len(outs) == 1 else jnp.concatenate(outs, axis=1)


def _unpack(p, grp):
    lo = lax.bitcast_convert_type(p << 16, F32)
    hi = lax.bitcast_convert_type(p & jnp.uint32(HI_MASK), F32)
    half = grp // 2
    outs = []
    for g in range(p.shape[1] // half):
        outs.append(lo[:, g * half:(g + 1) * half])
        outs.append(hi[:, g * half:(g + 1) * half])
    return jnp.concatenate(outs, axis=1)


def _rmsnorm_kernel(x_ref, g_ref, o_ref):
    o_ref[...] = _rms(x_ref[...].astype(F32), g_ref[...]).astype(o_ref.dtype)


def rmsnorm_rows(x, g, out_dtype):
    m, n = x.shape
    tm = _pick(m, (512, 256, 128, 64, 8))
    return pl.pallas_call(
        _rmsnorm_kernel,
        grid=(m // tm,),
        in_specs=[pl.BlockSpec((tm, n), lambda i: (i, 0)), pl.BlockSpec((1, n), lambda i: (0, 0))],
        out_specs=pl.BlockSpec((tm, n), lambda i: (i, 0)),
        out_shape=jax.ShapeDtypeStruct((m, n), out_dtype),
        name="rmsnorm_rows",
        compiler_params=_cparams("parallel"),
    )(x, g.reshape(1, n).astype(F32))


def _mm_kernel(*refs, has_bias, epilogue):
    x_ref, w_ref = refs[0], refs[1]
    acc = _dot(x_ref[...].astype(BF16), w_ref[...])
    k = 2
    if has_bias:
        acc = acc + refs[k][...]
        k += 1
    extras = [r[...] for r in refs[k:-1]]
    o_ref = refs[-1]
    o_ref[...] = epilogue(acc, *extras).astype(o_ref.dtype)


def matmul(x, w, bias=None, extras=(), epilogue=lambda a: a, out_dtype=F32, m_rows=None, tn=None, name="matmul"):
    m = x.shape[0] if m_rows is None else m_rows
    kdim, n = w.shape
    tm = _pick(m, (1024, 512, 256, 128, 64, 8))
    tn = tn or _pick(n, (512, 256, 128))
    in_specs = [pl.BlockSpec((tm, kdim), lambda i, j: (i, 0)), pl.BlockSpec((kdim, tn), lambda i, j: (0, j))]
    args = [x, w]
    if bias is not None:
        in_specs.append(pl.BlockSpec((1, tn), lambda i, j: (0, j)))
        args.append(bias.reshape(1, n).astype(F32))
    for arr, kind in extras:
        if kind == "tile":
            in_specs.append(pl.BlockSpec((tm, tn), lambda i, j: (i, j)))
        elif kind == "row":
            in_specs.append(pl.BlockSpec((1, tn), lambda i, j: (0, j)))
        else:
            in_specs.append(pl.BlockSpec((tm, arr.shape[1]), lambda i, j: (i, 0)))
        args.append(arr)
    return pl.pallas_call(
        functools.partial(_mm_kernel, has_bias=bias is not None, epilogue=epilogue),
        grid=(m // tm, n // tn),
        in_specs=in_specs,
        out_specs=pl.BlockSpec((tm, tn), lambda i, j: (i, j)),
        out_shape=jax.ShapeDtypeStruct((m, n), out_dtype),
        name=name,
        compiler_params=_cparams("parallel", "parallel"),
    )(*args)


def _rglru_kernel(xr_ref, yg_ref, conv0_ref, h0_ref, cw_ref, cb_ref, wa_ref, ba_ref, wx_ref, bx_ref,
                  lam_ref, *rest, tt, nblk, n_alias):
    hy_ref, hl_ref, cx, ch = rest[n_alias:]
    t = pl.program_id(2)

    @pl.when(t == 0)
    def _():
        cx[...] = conv0_ref[0]
        ch[...] = h0_ref[0]

    xr = xr_ref[...]
    xp = jnp.concatenate([cx[...], xr], axis=0)
    cw = cw_ref[...]
    xc = cb_ref[...]
    for k in range(CONV_W):
        off = SUBLANES - (CONV_W - 1) + k
        xc = xc + xp[off:off + tt] * cw[k:k + 1]
    cx[...] = xr[tt - SUBLANES:]

    ra, ia = [], []
    for b in range(nblk):
        xb = xc[:, b * LANES:(b + 1) * LANES].astype(BF16)
        ra.append(_dot(xb, wa_ref[b]))
        ia.append(_dot(xb, wx_ref[b]))
    cat = (lambda v: v[0] if len(v) == 1 else jnp.concatenate(v, axis=1))
    r = jax.nn.sigmoid(cat(ra) + ba_ref[...])
    gi = jax.nn.sigmoid(cat(ia) + bx_ref[...])
    log_a = -RG_C * r * jax.nn.softplus(-lam_ref[...])
    a = jnp.exp(log_a)
    u = jnp.sqrt(-jnp.tanh(log_a) * (1.0 + a * a)) * (gi * xc)

    row = lax.broadcasted_iota(I32, (tt, 1), 0)
    d = 1
    while d < tt:
        keep = row >= d
        a_s = jnp.where(keep, pltpu.roll(a, d, 0), 1.0)
        u_s = jnp.where(keep, pltpu.roll(u, d, 0), 0.0)
        u = a * u_s + u
        a = a * a_s
        d *= 2
    h = u + a * ch[0:1]
    last = h[tt - 1:tt]
    ch[...] = jnp.broadcast_to(last, ch.shape)
    hy_ref[...] = (h * yg_ref[...].astype(F32)).astype(hy_ref.dtype)
    hl_ref[0] = jnp.broadcast_to(last, hl_ref.shape[1:])


def rglru(xr, yg, conv0, h0, cw, cb, wa, ba, wx, bx, lam, *, nseq, seqlen, row0, hy_prev=None):
    t_all, d = xr.shape
    tt = _pick(seqlen, (256, 128, 64))
    nt = seqlen // tt
    dc = _pick(d, (512, 256, 128))
    nblk = dc // LANES
    rb0 = row0 // tt
    rowmap = lambda s, c, t: (rb0 + s * nt + t, c)
    vec = lambda s, c, t: (0, c)
    in_specs = [
        pl.BlockSpec((tt, dc), rowmap), pl.BlockSpec((tt, dc), rowmap),
        pl.BlockSpec((1, SUBLANES, dc), lambda s, c, t: (s, 0, c)),
        pl.BlockSpec((1, SUBLANES, dc), lambda s, c, t: (s, 0, c)),
        pl.BlockSpec((SUBLANES, dc), vec), pl.BlockSpec((1, dc), vec),
        pl.BlockSpec((nblk, LANES, LANES), lambda s, c, t: (c, 0, 0)), pl.BlockSpec((1, dc), vec),
        pl.BlockSpec((nblk, LANES, LANES), lambda s, c, t: (c, 0, 0)), pl.BlockSpec((1, dc), vec),
        pl.BlockSpec((1, dc), vec),
    ]
    args = [xr, yg, conv0, h0, cw, cb, wa, ba, wx, bx, lam]
    aliases = {}
    if hy_prev is not None:
        in_specs.append(pl.BlockSpec(memory_space=pl.ANY))
        args.append(hy_prev)
        aliases = {len(args) - 1: 0}
    return pl.pallas_call(
        functools.partial(_rglru_kernel, tt=tt, nblk=nblk, n_alias=len(aliases)),
        grid=(nseq, d // dc, nt),
        in_specs=in_specs,
        out_specs=[pl.BlockSpec((tt, dc), rowmap), pl.BlockSpec((1, SUBLANES, dc), lambda s, c, t: (s, 0, c))],
        out_shape=[jax.ShapeDtypeStruct((t_all, d), BF16), jax.ShapeDtypeStruct((nseq, SUBLANES, d), F32)],
        scratch_shapes=[pltpu.VMEM((SUBLANES, dc), F32), pltpu.VMEM((SUBLANES, dc), F32)],
        input_output_aliases=aliases,
        name="rglru",
        compiler_params=_cparams("parallel", "parallel", "arbitrary"),
    )(*args)


def _attn_prompt_kernel(q_ref, k_ref, v_ref, kr_ref, o_ref, *, tq, nsplit):
    qi = pl.program_id(2)
    q = q_ref[...]

    def keys(j0, n):
        return jnp.concatenate([k_ref[pl.ds(j0, n), :], kr_ref[pl.ds(j0, n), :].astype(BF16)], axis=1)

    def update(carry, s, v):
        m, l, acc = carry
        m_new = jnp.maximum(m, jnp.max(s, axis=1, keepdims=True))
        alpha = jnp.exp(m - m_new)
        p = jnp.exp(s - m_new)
        l = alpha * l + jnp.sum(p, axis=1, keepdims=True)
        acc = alpha * acc + _dot(p.astype(BF16), v)
        return m_new, l, acc

    def body(j, carry):
        j0 = pl.multiple_of(j * tq, tq)
        return update(carry, _dot_t(q, keys(j0, tq)), v_ref[pl.ds(j0, tq), :])

    init = (jnp.full((tq, 1), NEG, F32), jnp.zeros((tq, 1), F32), jnp.zeros((tq, LANES), F32))
    carry = lax.fori_loop(0, qi, body, init)
    j0 = pl.multiple_of(qi * tq, tq)
    th = tq // nsplit
    for i in range(nsplit):
        rows, n = slice(i * th, (i + 1) * th), (i + 1) * th
        qc = (lax.broadcasted_iota(I32, (th, n), 0) + i * th) // CHUNK
        kc = lax.broadcasted_iota(I32, (th, n), 1) // CHUNK
        s = jnp.where(kc <= qc, _dot_t(q[rows], keys(j0, n)), NEG)
        _, l, acc = update(tuple(c[rows] for c in carry), s, v_ref[pl.ds(j0, n), :])
        o_ref[rows, :] = (acc / l).astype(o_ref.dtype)


def attn_prompt(q, kv, krr, *, nbatch, seqlen, nheads):
    t_all = q.shape[0]
    tq = _pick(seqlen, (512, 256, 128, 64))
    nq = seqlen // tq
    return pl.pallas_call(
        functools.partial(_attn_prompt_kernel, tq=tq, nsplit=1),
        name="attn_prompt",
        grid=(nbatch, nheads, nq),
        in_specs=[
            pl.BlockSpec((tq, 2 * LANES), lambda b, h, i: (b * nq + i, h)),
            pl.BlockSpec((seqlen, LANES), lambda b, h, i: (b, h)),
            pl.BlockSpec((seqlen, LANES), lambda b, h, i: (b, nheads + h)),
            pl.BlockSpec((seqlen, LANES), lambda b, h, i: (b, 0)),
        ],
        out_specs=pl.BlockSpec((tq, LANES), lambda b, h, i: (b * nq + i, h)),
        out_shape=jax.ShapeDtypeStruct((t_all, nheads * LANES), BF16),
        compiler_params=_cparams("parallel", "parallel", "arbitrary"),
    )(q, kv, kv, krr)


def _attn_sample_kernel(q_ref, cn_ref, kr_ref, past_ref, pkr_ref, wukt_ref, wuv_ref, o_prev, o_ref, *, nheads):
    del o_prev
    q = q_ref[...]
    s_len = q.shape[0]
    lat = jnp.concatenate([past_ref[0].astype(BF16), cn_ref[...].astype(BF16)], axis=0)
    kro = jnp.concatenate([pkr_ref[0].astype(BF16), kr_ref[...].astype(BF16)], axis=0)
    kext = jnp.concatenate([lat, kro], axis=1)
    qs = []
    for h in range(nheads):
        qn = q[:, h * 2 * LANES:h * 2 * LANES + LANES]
        qr = q[:, h * 2 * LANES + LANES:(h + 1) * 2 * LANES]
        qs.append(jnp.concatenate([_dot(qn, wukt_ref[h]).astype(BF16), qr], axis=1))
    qext = jnp.concatenate(qs, axis=0)
    s = _dot_t(qext, kext)
    m = jnp.max(s, axis=1, keepdims=True)
    p = jnp.exp(s - m)
    l = jnp.sum(p, axis=1, keepdims=True)
    ol = (_dot(p.astype(BF16), lat) / l).astype(BF16)
    outs = [_dot(ol[h * s_len:(h + 1) * s_len], wuv_ref[h]) for h in range(nheads)]
    o_ref[...] = jnp.concatenate(outs, axis=1).astype(o_ref.dtype)


def attn_sample(q, ckv, krr, past_ckv, past_kr2, wukt, wuv, o_prev, *, nbatch, seqlen, row0, nheads):
    t_all = q.shape[0]
    rb0 = row0 // seqlen
    p_len, c = past_ckv.shape[1:]
    return pl.pallas_call(
        functools.partial(_attn_sample_kernel, nheads=nheads),
        grid=(nbatch,),
        in_specs=[
            pl.BlockSpec((seqlen, nheads * 2 * LANES), lambda b: (rb0 + b, 0)),
            pl.BlockSpec((seqlen, c), lambda b: (rb0 + b, 0)),
            pl.BlockSpec((seqlen, LANES), lambda b: (rb0 + b, 0)),
            pl.BlockSpec((1, p_len, c), lambda b: (b, 0, 0)),
            pl.BlockSpec((1, p_len, LANES), lambda b: (b, 0, 0)),
            pl.BlockSpec(wukt.shape, lambda b: (0, 0, 0)),
            pl.BlockSpec(wuv.shape, lambda b: (0, 0, 0)),
            pl.BlockSpec(memory_space=pl.ANY),
        ],
        out_specs=pl.BlockSpec((seqlen, nheads * LANES), lambda b: (rb0 + b, 0)),
        out_shape=jax.ShapeDtypeStruct((t_all, nheads * LANES), BF16),
        input_output_aliases={7: 0},
        name="attn_sample",
        compiler_params=_cparams("parallel"),
    )(q, ckv, krr, past_ckv, past_kr2, wukt, wuv, o_prev)


def _xattn_kernel(x1_ref, gx_ref, wq_ref, mk_ref, mv_ref, wo_ref, gf_ref, rwh_ref, rwl_ref, rb_ref, cnt0_ref,
                  *rest, nb, seg, xheads, n_alias):
    x2_ref, hp_ref, ir_ref, gw_ref, cnt_ref, carry = rest[n_alias:]
    i = pl.program_id(0)

    @pl.when(i == 0)
    def _():
        carry[...] = cnt0_ref[...]

    x1 = x1_ref[...]
    tm = x1.shape[0]
    hn = _rms(x1, gx_ref[...]).astype(BF16)
    q = _dot(hn, wq_ref[...]).astype(BF16)
    scale = LANES ** -0.5
    segs = []
    for n in range(nb):
        heads = []
        for h in range(xheads):
            cols = slice(h * LANES, (h + 1) * LANES)
            qh = q[n * seg:(n + 1) * seg, cols]
            kh = mk_ref[n][:, cols].astype(BF16)
            vh = mv_ref[n][:, cols].astype(BF16)
            s = _dot_t(qh, kh) * scale
            m = jnp.max(s, axis=1, keepdims=True)
            p = jnp.exp(s - m)
            l = jnp.sum(p, axis=1, keepdims=True)
            heads.append(_dot(p.astype(BF16), vh) / l)
        segs.append(jnp.concatenate(heads, axis=1))
    o = (segs[0] if nb == 1 else jnp.concatenate(segs, axis=0)).astype(BF16)
    x2 = x1 + _dot(o, wo_ref[...])
    x2_ref[...] = x2

    hf = _rms(x2, gf_ref[...])
    hp_ref[...] = _pack(hf, hf.shape[1])
    hb = hf.astype(BF16)
    hl = (hf - hb.astype(F32)).astype(BF16)
    logits = _dot(hb, rwh_ref[...]) + _dot(hl, rwh_ref[...]) + _dot(hb, rwl_ref[...]) + rb_ref[...]

    lane = lax.broadcasted_iota(I32, (tm, LANES), 1)
    work = logits
    sels, vals, idxs = [], [], []
    for _ in range(TOP_K):
        m = jnp.max(work, axis=1, keepdims=True)
        idx = jnp.min(jnp.where(work == m, lane, LANES), axis=1, keepdims=True)
        sel = lane == idx
        sels.append(sel)
        vals.append(m)
        idxs.append(idx)
        work = jnp.where(sel, -jnp.inf, work)
    es = [jnp.exp(v - vals[0]) for v in vals]
    den = es[0] + es[1] + es[2] + es[3]
    onehot = (sels[0] | sels[1] | sels[2] | sels[3]).astype(F32)
    tri = (lax.broadcasted_iota(I32, (tm, tm), 0) > lax.broadcasted_iota(I32, (tm, tm), 1)).astype(BF16)
    excl = _dot(tri, onehot.astype(BF16)) + carry[0:1]
    carry[...] = carry[...] + jnp.sum(onehot, axis=0, keepdims=True)
    ir = jnp.zeros((tm, LANES), I32)
    gw = jnp.zeros((tm, LANES), F32)
    for k in range(TOP_K):
        rank = jnp.sum(jnp.where(sels[k], excl, 0.0), axis=1, keepdims=True).astype(I32)
        ir = jnp.where(lane == k, idxs[k], ir)
        ir = jnp.where(lane == TOP_K + k, rank, ir)
        gw = jnp.where(lane == k, es[k] / den, gw)
    ir_ref[...] = ir
    gw_ref[...] = gw
    cnt_ref[...] = carry[...]


def xattn_router(x1, gx, wq, mk, mv, wo, gf, rwh, rwl, rb, cnt0, *, nb, seg, ntiles, row0, xheads, prev=None):
    t_all, d = x1.shape
    tm = nb * seg
    rb0 = row0 // tm
    rows = lambda i: (rb0 + i, 0)
    const = lambda i: (0, 0)
    mem_rows, mem_cols = mk.shape[1:]
    if nb == 1:
        tiles_per_batch = ntiles // mk.shape[0]
        mem_map = lambda i: (i // tiles_per_batch, 0, 0)
    else:
        mem_map = lambda i: (i, 0, 0)
    in_specs = [
        pl.BlockSpec((tm, d), rows), pl.BlockSpec((1, d), const), pl.BlockSpec(wq.shape, const),
        pl.BlockSpec((nb, mem_rows, mem_cols), mem_map), pl.BlockSpec((nb, mem_rows, mem_cols), mem_map),
        pl.BlockSpec(wo.shape, const), pl.BlockSpec((1, d), const),
        pl.BlockSpec(rwh.shape, const), pl.BlockSpec(rwl.shape, const), pl.BlockSpec((1, LANES), const),
        pl.BlockSpec((SUBLANES, LANES), const),
    ]
    args = [x1, gx, wq, mk, mv, wo, gf, rwh, rwl, rb, cnt0]
    aliases = {}
    if prev is not None:
        for k, arr in enumerate(prev):
            in_specs.append(pl.BlockSpec(memory_space=pl.ANY))
            args.append(arr)
            aliases[len(args) - 1] = k
    out_shape = [
        jax.ShapeDtypeStruct((t_all, d), F32), jax.ShapeDtypeStruct((t_all, d // 2), U32),
        jax.ShapeDtypeStruct((t_all, LANES), I32), jax.ShapeDtypeStruct((t_all, LANES), F32),
        jax.ShapeDtypeStruct((SUBLANES, LANES), F32),
    ]
    out_specs = [
        pl.BlockSpec((tm, d), rows), pl.BlockSpec((tm, d // 2), rows),
        pl.BlockSpec((tm, LANES), rows), pl.BlockSpec((tm, LANES), rows),
        pl.BlockSpec((SUBLANES, LANES), const),
    ]
    return pl.pallas_call(
        functools.partial(_xattn_kernel, nb=nb, seg=seg, xheads=xheads, n_alias=len(aliases)),
        grid=(ntiles,),
        in_specs=in_specs,
        out_specs=out_specs,
        out_shape=out_shape,
        scratch_shapes=[pltpu.VMEM((SUBLANES, LANES), F32)],
        input_output_aliases=aliases,
        name="xattn_router",
        compiler_params=_cparams("arbitrary"),
    )(*args)


def _row_copy(src, dst, i, j, sem):
    return pltpu.make_async_copy(src.at[pl.ds(i, 1)], dst.at[pl.ds(j, 1)], sem)


def _token_row_copies(n_tokens, copy, same_size_copy):
    def issue(t, c):
        for k in range(TOP_K):
            copy(t, k).start()
        return c

    lax.fori_loop(0, n_tokens, issue, 0, unroll=DMA_UNROLL)

    def drain(t, c):
        for _ in range(TOP_K):
            same_size_copy.wait()
        return c

    lax.fori_loop(0, n_tokens, drain, 0, unroll=DMA_UNROLL)


def _dispatch_kernel(offs_ref, ir_ref, h_ref, xb_in, xb_out, sem, *, tmd):
    del xb_in

    def copy(t, k):
        dst = offs_ref[ir_ref[k, t]] + ir_ref[TOP_K + k, t]
        return _row_copy(h_ref, xb_out, t, dst, sem.at[0])

    _token_row_copies(tmd, copy, _row_copy(h_ref, xb_out, 0, 0, sem.at[0]))


def moe_dispatch(offs, ir_t, hp, n_rows):
    t_all, w = hp.shape
    tmd = _pick(t_all, (512, 256, 128))
    xb0 = jnp.zeros((n_rows, w), U32)
    return pl.pallas_call(
        functools.partial(_dispatch_kernel, tmd=tmd),
        grid_spec=pltpu.PrefetchScalarGridSpec(
            num_scalar_prefetch=1,
            grid=(t_all // tmd,),
            in_specs=[
                pl.BlockSpec((2 * TOP_K, tmd), lambda i, offs: (0, i), memory_space=pltpu.SMEM),
                pl.BlockSpec((tmd, w), lambda i, offs: (i, 0)),
                pl.BlockSpec(memory_space=pl.ANY),
            ],
            out_specs=pl.BlockSpec(memory_space=pl.ANY),
            scratch_shapes=[pltpu.SemaphoreType.DMA((1,))],
        ),
        out_shape=jax.ShapeDtypeStruct((n_rows, w), U32),
        input_output_aliases={3: 0},
        name="moe_dispatch",
        compiler_params=_cparams("arbitrary"),
    )(offs, ir_t, hp, xb0)


def _expert_up_kernel(se, sj, si, sf, sv, xb_ref, wg_ref, wu_ref, bg_ref, bu_ref, o_ref, wgb, wub):
    s = pl.program_id(0)

    @pl.when(sf[s] == 1)
    def _():
        wgb[...] = wg_ref[0].astype(BF16)
        wub[...] = wu_ref[0].astype(BF16)

    @pl.when(sv[s] == 1)
    def _():
        p = xb_ref[...]
        x = _unpack(p, 2 * p.shape[1]).astype(BF16)
        g = jnp.minimum(_dot(x, wgb[...]) + bg_ref[0], SWIGLU_LIMIT)
        u = jnp.clip(_dot(x, wub[...]) + bu_ref[0], -SWIGLU_LIMIT, SWIGLU_LIMIT)
        o_ref[...] = ((u + 1.0) * g * jax.nn.sigmoid(SWIGLU_ALPHA * g)).astype(o_ref.dtype)


def _expert_down_kernel(se, sj, si, sf, sv, a_ref, wd_ref, bd_ref, o_ref, wdb):
    s = pl.program_id(0)

    @pl.when(sf[s] == 1)
    def _():
        wdb[...] = wd_ref[0].astype(BF16)

    @pl.when(sv[s] == 1)
    def _():
        y = _dot(a_ref[...], wdb[...]) + bd_ref[0]
        o_ref[...] = _pack(y, y.shape[1])


def _expert_schedule(nblk_e, nj, nb_max):
    cum = jnp.cumsum(nblk_e)
    start = cum - nblk_e
    total = cum[-1]
    ns = nj * nb_max
    s = jnp.minimum(jnp.arange(ns, dtype=I32), nj * total - 1)
    e = jnp.sum((s[:, None] >= nj * cum[None, :]).astype(I32), axis=1)
    local = s - nj * start[e]
    n_e = jnp.maximum(nblk_e[e], 1)
    j = local // n_e
    r = local % n_e
    valid = (jnp.arange(ns, dtype=I32) < nj * total).astype(I32)
    first = ((r == 0).astype(I32)) * valid
    return e, j.astype(I32), (start[e] + r).astype(I32), first, valid


def moe_experts(xb, nblk_e, w_gate, b_gate, w_up, b_up, w_down, b_down, *, tme):
    n_rows, half = xb.shape
    d = 2 * half
    ne, _, dff = w_gate.shape
    nb_max = n_rows // tme
    tn_up = _pick(dff, (512, 256, 128))
    nj = dff // tn_up
    sched = _expert_schedule(nblk_e, nj, nb_max)
    act = pl.pallas_call(
        _expert_up_kernel,
        grid_spec=pltpu.PrefetchScalarGridSpec(
            num_scalar_prefetch=5,
            grid=(nj * nb_max,),
            in_specs=[
                pl.BlockSpec((tme, half), lambda s, se, sj, si, sf, sv: (si[s], 0)),
                pl.BlockSpec((1, d, tn_up), lambda s, se, sj, si, sf, sv: (se[s], 0, sj[s])),
                pl.BlockSpec((1, d, tn_up), lambda s, se, sj, si, sf, sv: (se[s], 0, sj[s])),
                pl.BlockSpec((1, 1, tn_up), lambda s, se, sj, si, sf, sv: (se[s], 0, sj[s])),
                pl.BlockSpec((1, 1, tn_up), lambda s, se, sj, si, sf, sv: (se[s], 0, sj[s])),
            ],
            out_specs=pl.BlockSpec((tme, tn_up), lambda s, se, sj, si, sf, sv: (si[s], sj[s])),
            scratch_shapes=[pltpu.VMEM((d, tn_up), BF16), pltpu.VMEM((d, tn_up), BF16)],
        ),
        out_shape=jax.ShapeDtypeStruct((n_rows, dff), BF16),
        name="expert_up",
        compiler_params=_cparams("arbitrary"),
    )(*sched, xb, w_gate, w_up, b_gate.reshape(ne, 1, dff), b_up.reshape(ne, 1, dff))

    tn_dn = _pick(d, (1024, 512, 256))
    nj2 = d // tn_dn
    sched2 = _expert_schedule(nblk_e, nj2, nb_max)
    yb = pl.pallas_call(
        _expert_down_kernel,
        grid_spec=pltpu.PrefetchScalarGridSpec(
            num_scalar_prefetch=5,
            grid=(nj2 * nb_max,),
            in_specs=[
                pl.BlockSpec((tme, dff), lambda s, se, sj, si, sf, sv: (si[s], 0)),
                pl.BlockSpec((1, dff, tn_dn), lambda s, se, sj, si, sf, sv: (se[s], 0, sj[s])),
                pl.BlockSpec((1, 1, tn_dn), lambda s, se, sj, si, sf, sv: (se[s], 0, sj[s])),
            ],
            out_specs=pl.BlockSpec((tme, tn_dn // 2), lambda s, se, sj, si, sf, sv: (si[s], sj[s])),
            scratch_shapes=[pltpu.VMEM((dff, tn_dn), BF16)],
        ),
        out_shape=jax.ShapeDtypeStruct((n_rows, d // 2), U32),
        name="expert_down",
        compiler_params=_cparams("arbitrary"),
    )(*sched2, act, w_down, b_down.reshape(ne, 1, d))
    return yb, tn_dn


def _combine_kernel(offs_ref, ir_ref, gw_ref, x2_ref, g_ref, yb_hbm, o_ref, buf, sem, *, tmc, grp):
    def copy(t, k):
        src = offs_ref[ir_ref[k, t]] + ir_ref[TOP_K + k, t]
        return _row_copy(yb_hbm, buf.at[k], src, t, sem.at[0])

    _token_row_copies(tmc, copy, _row_copy(yb_hbm, buf.at[0], 0, 0, sem.at[0]))
    y = x2_ref[...]
    gw = gw_ref[...]
    for k in range(TOP_K):
        y = y + gw[:, k:k + 1] * _unpack(buf[k], grp)
    o_ref[...] = _rms(y, g_ref[...])


def moe_combine(offs, ir_t, gw, x2, g, yb, *, rows, row0, grp):
    d = x2.shape[1]
    tmc = _pick(rows, (256, 128))
    rb0 = row0 // tmc
    return pl.pallas_call(
        functools.partial(_combine_kernel, tmc=tmc, grp=grp),
        grid_spec=pltpu.PrefetchScalarGridSpec(
            num_scalar_prefetch=1,
            grid=(rows // tmc,),
            in_specs=[
                pl.BlockSpec((2 * TOP_K, tmc), lambda i, offs: (0, rb0 + i), memory_space=pltpu.SMEM),
                pl.BlockSpec((tmc, LANES), lambda i, offs: (rb0 + i, 0)),
                pl.BlockSpec((tmc, d), lambda i, offs: (rb0 + i, 0)),
                pl.BlockSpec((1, d), lambda i, offs: (0, 0)),
                pl.BlockSpec(memory_space=pl.ANY),
            ],
            out_specs=pl.BlockSpec((tmc, d), lambda i, offs: (i, 0)),
            scratch_shapes=[pltpu.VMEM((TOP_K, tmc, d // 2), U32), pltpu.SemaphoreType.DMA((1,))],
        ),
        out_shape=jax.ShapeDtypeStruct((rows, d), F32),
        name="moe_combine",
        compiler_params=_cparams("arbitrary"),
    )(offs, ir_t, gw, x2, g.reshape(1, d).astype(F32), yb)


def _rot_cols(w):
    half = w.shape[-1] // 2
    return jnp.concatenate([-w[..., half:], w[..., :half]], axis=-1)


def kernel(x_prompt, x_sample, mem_prompt, cache_mla_ckv, cache_mla_krope, cache_mem_k, cache_mem_v, state_rg_h, state_rg_conv, norm_mix_g, w_in, b_in, rg_conv_w, rg_conv_b, rg_a_w, rg_a_b, rg_x_w, rg_x_b, rg_lambda, w_rnn_proj, mla_q_norm_g, mla_w_uq_nope, mla_w_uq_rope, mla_kv_norm_g, mla_w_uk, mla_w_uv, w_mla_proj, w_out, norm_x_g, norm_mem_g, w_xq, w_xk, w_xv, w_xo, norm_ffn_g, router_w, router_b, w_gate, b_gate, w_up, b_up, w_down, b_down, norm_final_g):
    nb_p, s_p, d = x_prompt.shape
    nb_s, s_s, _ = x_sample.shape
    assert w_in.shape[0] == 1, "single-layer step"
    t_p, t_s = nb_p * s_p, nb_s * s_s
    t_all = t_p + t_s
    d_rnn = rg_lambda.shape[-1]
    q_lora, kv_lora = mla_q_norm_g.shape[-1], mla_kv_norm_g.shape[-1]
    nheads, qk_nope = mla_w_uk.shape[2:]
    rope_dim = cache_mla_krope.shape[-1]
    v_dim = mla_w_uv.shape[-1]
    p_len = cache_mla_ckv.shape[2]
    n_mem, xheads, xhd = cache_mem_k.shape[2:]
    ne = router_w.shape[-1]
    assert qk_nope == LANES and v_dim == LANES and xhd == LANES and 2 * rope_dim == LANES
    assert p_len % CHUNK == 0 and s_s <= CHUNK and ne <= LANES

    wi, bi = w_in[0], b_in[0]
    o1, o2 = d_rnn, 2 * d_rnn
    o3, o4 = o2 + q_lora, o2 + q_lora + kv_lora
    o5 = o4 + rope_dim
    o6 = o5 + d
    seg = lambda a, b: (wi[:, a:b].astype(BF16), bi[a:b])
    w_xr, b_xr = seg(0, o1)
    w_yr, b_yr = seg(o1, o2)
    w_cq, b_cq = seg(o2, o3)
    w_ckv, b_ckv = seg(o3, o4)
    w_gr, b_gr = seg(o5, o6)
    w_gm, b_gm = seg(o6, o6 + d)
    w_kr = jnp.concatenate([wi[:, o4:o5], _rot_cols(wi[:, o4:o5])], axis=1).astype(BF16)
    b_kr = jnp.concatenate([bi[o4:o5], _rot_cols(bi[o4:o5])])
    w_q = jnp.concatenate([mla_w_uq_nope[0], mla_w_uq_rope[0], _rot_cols(mla_w_uq_rope[0])], axis=-1)
    w_q = w_q.reshape(q_lora, nheads * 2 * LANES).astype(BF16)
    w_kv = jnp.concatenate([mla_w_uk[0].reshape(kv_lora, -1), mla_w_uv[0].reshape(kv_lora, -1)], axis=1).astype(BF16)
    w_ukt = jnp.transpose(mla_w_uk[0], (1, 2, 0)).astype(BF16)
    w_uvh = jnp.transpose(mla_w_uv[0], (1, 0, 2)).astype(BF16)

    half = rope_dim // 2
    freq = ROPE_BASE ** (-jnp.arange(half, dtype=F32) / half)
    pos = jnp.concatenate([jnp.tile(jnp.arange(s_p, dtype=I32), nb_p),
                           jnp.tile(p_len + jnp.arange(s_s, dtype=I32), nb_s)]).astype(F32)
    ang = pos[:, None] * freq[None, :]
    cs, sn = jnp.cos(ang), jnp.sin(ang)
    ktab = jnp.concatenate([cs, cs, sn, sn], axis=1)
    mla_scale = float(qk_nope + rope_dim) ** -0.5
    qtab = jnp.concatenate([jnp.ones((t_all, LANES), F32), ktab], axis=1) * mla_scale

    x_all = jnp.concatenate([x_prompt.reshape(t_p, d), x_sample.reshape(t_s, d)], axis=0)
    hn = rmsnorm_rows(x_all, norm_mix_g[0], BF16)
    xr = matmul(hn, w_xr, b_xr)
    yg = matmul(hn, w_yr, b_yr, epilogue=jax.nn.gelu, out_dtype=BF16)
    sg_rnn = matmul(hn, w_gr, b_gr, epilogue=jax.nn.sigmoid, out_dtype=BF16)
    sg_mla = matmul(hn, w_gm, b_gm, epilogue=jax.nn.sigmoid, out_dtype=BF16)
    cqn = matmul(hn, w_cq, b_cq, extras=[(mla_q_norm_g[0].reshape(1, -1), "row")], epilogue=_rms,
                 out_dtype=BF16, tn=q_lora)
    ckv = matmul(hn, w_ckv, b_ckv, extras=[(mla_kv_norm_g[0].reshape(1, -1), "row")], epilogue=_rms, tn=kv_lora)

    def rope_epilogue(acc, tab):
        z = acc * tab
        return z + pltpu.roll(z, rope_dim, 1)

    krr = matmul(hn, w_kr, b_kr, extras=[(ktab, "col")], epilogue=rope_epilogue, tn=LANES)

    cw = jnp.pad(rg_conv_w[0], ((0, SUBLANES - CONV_W), (0, 0)))
    cb = rg_conv_b[0].reshape(1, d_rnn)
    wa, wx = rg_a_w[0].astype(BF16), rg_x_w[0].astype(BF16)
    ba, bx = rg_a_b[0].reshape(1, d_rnn), rg_x_b[0].reshape(1, d_rnn)
    lam = rg_lambda[0].reshape(1, d_rnn)
    zeros_state = jnp.zeros((nb_p, SUBLANES, d_rnn), F32)
    hy, hl_p = rglru(xr, yg, zeros_state, zeros_state, cw, cb, wa, ba, wx, bx, lam,
                     nseq=nb_p, seqlen=s_p, row0=0)
    conv_s = jnp.pad(state_rg_conv[0], ((0, 0), (SUBLANES - (CONV_W - 1), 0), (0, 0)))
    h0_s = jnp.broadcast_to(state_rg_h[0][:, None, :], (nb_s, SUBLANES, d_rnn)).astype(F32)
    hy, hl_s = rglru(xr, yg, conv_s, h0_s, cw, cb, wa, ba, wx, bx, lam,
                     nseq=nb_s, seqlen=s_s, row0=t_p, hy_prev=hy)
    m1 = matmul(hy, w_rnn_proj[0].astype(BF16), extras=[(sg_rnn, "tile")],
                epilogue=lambda a, g: a * g.astype(F32), out_dtype=BF16)

    def q_epilogue(acc, tab):
        reps = acc.shape[1] // tab.shape[1]
        return acc * (tab if reps == 1 else jnp.concatenate([tab] * reps, axis=1))

    q = matmul(cqn, w_q, extras=[(qtab, "col")], epilogue=q_epilogue, out_dtype=BF16,
               tn=_pick(nheads * 2 * LANES, (1024, 512, 256)))
    kvp = matmul(ckv, w_kv, out_dtype=BF16, m_rows=t_p)
    o = attn_prompt(q, kvp, krr, nbatch=nb_p, seqlen=s_p, nheads=nheads)
    past_kr2 = jnp.concatenate([cache_mla_krope[0], cache_mla_krope[0]], axis=-1)
    o = attn_sample(q, ckv, krr, cache_mla_ckv[0], past_kr2, w_ukt, w_uvh, o,
                    nbatch=nb_s, seqlen=s_s, row0=t_p, nheads=nheads)
    merged = matmul(o, w_mla_proj[0].astype(BF16), extras=[(sg_mla, "tile"), (m1, "tile")],
                    epilogue=lambda a, g, r: a * g.astype(F32) + r.astype(F32), out_dtype=BF16)
    x1 = matmul(merged, w_out[0].astype(BF16), extras=[(x_all, "tile")], epilogue=lambda a, r: a + r)

    xcols = xheads * xhd
    mn = rmsnorm_rows(mem_prompt.reshape(nb_p * n_mem, d), norm_mem_g[0], BF16)
    mk = matmul(mn, w_xk[0].reshape(d, xcols).astype(BF16))
    mv = matmul(mn, w_xv[0].reshape(d, xcols).astype(BF16))
    rw = jnp.pad(router_w[0], ((0, 0), (0, LANES - ne)))
    rwh = rw.astype(BF16)
    rwl = (rw - rwh.astype(F32)).astype(BF16)
    rb = jnp.pad(router_b[0].astype(F32), (0, LANES - ne), constant_values=NEG).reshape(1, LANES)
    wq_x = w_xq[0].reshape(d, xcols).astype(BF16)
    wo_x = w_xo[0].reshape(xcols, d).astype(BF16)
    gx, gf = norm_x_g[0].reshape(1, d), norm_ffn_g[0].reshape(1, d)
    tile_p = _pick(s_p, (512, 256, 128, 64))
    outs = xattn_router(x1, gx, wq_x, mk.reshape(nb_p, n_mem, xcols), mv.reshape(nb_p, n_mem, xcols), wo_x, gf,
                        rwh, rwl, rb, jnp.zeros((SUBLANES, LANES), F32),
                        nb=1, seg=tile_p, ntiles=t_p // tile_p, row0=0, xheads=xheads)
    nb_x = _pick(nb_s, (8, 4, 2, 1))
    outs = xattn_router(x1, gx, wq_x, cache_mem_k[0].reshape(nb_s, n_mem, xcols),
                        cache_mem_v[0].reshape(nb_s, n_mem, xcols), wo_x, gf, rwh, rwl, rb, outs[4],
                        nb=nb_x, seg=s_s, ntiles=nb_s // nb_x, row0=t_p, xheads=xheads, prev=outs[:4])
    x2, hp, ir, gw, cnt = outs

    tme = 512
    counts = cnt[0, :ne].astype(I32)
    nblk_e = (counts + tme - 1) // tme
    offs = ((jnp.cumsum(nblk_e) - nblk_e) * tme).astype(I32)
    n_rows = (t_all * TOP_K // tme + ne) * tme
    ir_t = jnp.transpose(ir[:, :2 * TOP_K])
    xb = moe_dispatch(offs, ir_t, hp, n_rows)
    yb, grp = moe_experts(xb, nblk_e, w_gate[0], b_gate[0], w_up[0], b_up[0], w_down[0], b_down[0], tme=tme)
    y_p = moe_combine(offs, ir_t, gw, x2, norm_final_g, yb, rows=t_p, row0=0, grp=grp)
    y_s = moe_combine(offs, ir_t, gw, x2, norm_final_g, yb, rows=t_s, row0=t_p, grp=grp)

    nconv = CONV_W - 1
    xr_p = xr[:t_p].reshape(nb_p, s_p, d_rnn)
    xr_s = xr[t_p:].reshape(nb_s, s_s, d_rnn)
    conv_p = xr_p[:, s_p - nconv:]
    conv_new_s = jnp.concatenate([state_rg_conv[0], xr_s], axis=1)[:, -nconv:]
    return (
        y_p.reshape(nb_p, s_p, d), y_s.reshape(nb_s, s_s, d),
        hl_p[None, :, 0, :], conv_p[None],
        ckv[:t_p].reshape(1, nb_p, s_p, kv_lora), krr[:t_p, :rope_dim].reshape(1, nb_p, s_p, rope_dim),
        mk.reshape(1, nb_p, n_mem, xheads, xhd), mv.reshape(1, nb_p, n_mem, xheads, xhd),
        hl_s[None, :, 0, :], conv_new_s[None],
        ckv[t_p:].reshape(1, nb_s, s_s, kv_lora), krr[t_p:, :rope_dim].reshape(1, nb_s, s_s, rope_dim),
    )
```

```python
import functools

import jax
import jax.numpy as jnp
from jax import lax
from jax.experimental import pallas as pl
from jax.experimental.pallas import tpu as pltpu

F32, BF16, I32, U32 = jnp.float32, jnp.bfloat16, jnp.int32, jnp.uint32

CHUNK = 64
NORM_EPS = 1e-6
RG_C = 8.0
CONV_W = 4
ROPE_BASE = 10000.0
SWIGLU_LIMIT = 7.0
SWIGLU_ALPHA = 1.702
TOP_K = 4
LANES = 128
SUBLANES = 8
VMEM_LIMIT = 56 * 1024 * 1024
DMA_UNROLL = 8
HI_MASK = 0xFFFF0000
NEG = -1e30
LOG2_E = 1.4426950408889634


def _pick(n, prefs):
    for p in prefs:
        if n % p == 0:
            return p
    raise ValueError(f"no tile for {n} in {prefs}")


def _cparams(*sem):
    return pltpu.CompilerParams(dimension_semantics=sem, vmem_limit_bytes=VMEM_LIMIT)


def _rms(x, g):
    return x * lax.rsqrt(jnp.mean(x * x, axis=-1, keepdims=True) + NORM_EPS) * g


def _dot(a, b):
    return jnp.dot(a, b, preferred_element_type=F32)


def _dot_t(a, b):
    return lax.dot_general(a, b, (((1,), (1,)), ((), ())), preferred_element_type=F32)


def _pack(x, grp):
    bits = lax.bitcast_convert_type(x.astype(BF16).astype(F32), U32)
    half = grp // 2
    outs = []
    for g in range(x.shape[1] // grp):
        lo = bits[:, g * grp:g * grp + half]
        hi = bits[:, g * grp + half:(g + 1) * grp]
        outs.append((lo >> 16) | (hi & jnp.uint32(HI_MASK)))
    return outs[0] if len(outs) == 1 else jnp.concatenate(outs, axis=1)


def _unpack(p, grp):
    lo = lax.bitcast_convert_type(p << 16, F32)
    hi = lax.bitcast_convert_type(p & jnp.uint32(HI_MASK), F32)
    half = grp // 2
    outs = []
    for g in range(p.shape[1] // half):
        outs.append(lo[:, g * half:(g + 1) * half])
        outs.append(hi[:, g * half:(g + 1) * half])
    return jnp.concatenate(outs, axis=1)


def _rmsnorm_kernel(x_ref, g_ref, o_ref):
    o_ref[...] = _rms(x_ref[...].astype(F32), g_ref[...]).astype(o_ref.dtype)


def rmsnorm_rows(x, g, out_dtype):
    m, n = x.shape
    tm = _pick(m, (512, 256, 128, 64, 8))
    return pl.pallas_call(
        _rmsnorm_kernel,
        grid=(m // tm,),
        in_specs=[pl.BlockSpec((tm, n), lambda i: (i, 0)), pl.BlockSpec((1, n), lambda i: (0, 0))],
        out_specs=pl.BlockSpec((tm, n), lambda i: (i, 0)),
        out_shape=jax.ShapeDtypeStruct((m, n), out_dtype),
        name="rmsnorm_rows",
        compiler_params=_cparams("parallel"),
    )(x, g.reshape(1, n).astype(F32))


def _mm_kernel(*refs, has_bias, epilogue):
    x_ref, w_ref = refs[0], refs[1]
    acc = _dot(x_ref[...].astype(BF16), w_ref[...])
    k = 2
    if has_bias:
        acc = acc + refs[k][...]
        k += 1
    extras = [r[...] for r in refs[k:-1]]
    o_ref = refs[-1]
    o_ref[...] = epilogue(acc, *extras).astype(o_ref.dtype)


def matmul(x, w, bias=None, extras=(), epilogue=lambda a: a, out_dtype=F32, m_rows=None, tn=None, name="matmul"):
    m = x.shape[0] if m_rows is None else m_rows
    kdim, n = w.shape
    tm = _pick(m, (1024, 512, 256, 128, 64, 8))
    tn = tn or _pick(n, (1024, 512, 256, 128))
    in_specs = [pl.BlockSpec((tm, kdim), lambda i, j: (i, 0)), pl.BlockSpec((kdim, tn), lambda i, j: (0, j))]
    args = [x, w]
    if bias is not None:
        in_specs.append(pl.BlockSpec((1, tn), lambda i, j: (0, j)))
        args.append(bias.reshape(1, n).astype(F32))
    for arr, kind in extras:
        if kind == "tile":
            in_specs.append(pl.BlockSpec((tm, tn), lambda i, j: (i, j)))
        elif kind == "row":
            in_specs.append(pl.BlockSpec((1, tn), lambda i, j: (0, j)))
        else:
            in_specs.append(pl.BlockSpec((tm, arr.shape[1]), lambda i, j: (i, 0)))
        args.append(arr)
    return pl.pallas_call(
        functools.partial(_mm_kernel, has_bias=bias is not None, epilogue=epilogue),
        grid=(m // tm, n // tn),
        in_specs=in_specs,
        out_specs=pl.BlockSpec((tm, tn), lambda i, j: (i, j)),
        out_shape=jax.ShapeDtypeStruct((m, n), out_dtype),
        name=name,
        compiler_params=_cparams("parallel", "parallel"),
    )(*args)


def _rglru_kernel(xr_ref, yg_ref, conv0_ref, h0_ref, cw_ref, cb_ref, wa_ref, ba_ref, wx_ref, bx_ref,
                  lam_ref, *rest, tt, nblk, n_alias):
    hy_ref, hl_ref, cx, ch = rest[n_alias:]
    t = pl.program_id(2)

    @pl.when(t == 0)
    def _():
        cx[...] = conv0_ref[0]
        ch[...] = h0_ref[0]

    xr = xr_ref[...]
    xp = jnp.concatenate([cx[...], xr], axis=0)
    cw = cw_ref[...]
    xc = cb_ref[...]
    for k in range(CONV_W):
        off = SUBLANES - (CONV_W - 1) + k
        xc = xc + xp[off:off + tt] * cw[k:k + 1]
    cx[...] = xr[tt - SUBLANES:]

    ra, ia = [], []
    for b in range(nblk):
        xb = xc[:, b * LANES:(b + 1) * LANES].astype(BF16)
        ra.append(_dot(xb, wa_ref[b]))
        ia.append(_dot(xb, wx_ref[b]))
    cat = (lambda v: v[0] if len(v) == 1 else jnp.concatenate(v, axis=1))
    r = jax.nn.sigmoid(cat(ra) + ba_ref[...])
    gi = jax.nn.sigmoid(cat(ia) + bx_ref[...])
    log_a = -RG_C * r * jax.nn.softplus(-lam_ref[...])
    a = jnp.exp(log_a)
    u = jnp.sqrt(-jnp.tanh(log_a) * (1.0 + a * a)) * (gi * xc)

    row = lax.broadcasted_iota(I32, (tt, 1), 0)
    d = 1
    while d < tt:
        keep = row >= d
        a_s = jnp.where(keep, pltpu.roll(a, d, 0), 1.0)
        u_s = jnp.where(keep, pltpu.roll(u, d, 0), 0.0)
        u = a * u_s + u
        a = a * a_s
        d *= 2
    h = u + a * ch[0:1]
    last = h[tt - 1:tt]
    ch[...] = jnp.broadcast_to(last, ch.shape)
    hy_ref[...] = (h * yg_ref[...].astype(F32)).astype(hy_ref.dtype)
    hl_ref[0] = jnp.broadcast_to(last, hl_ref.shape[1:])


def rglru(xr, yg, conv0, h0, cw, cb, wa, ba, wx, bx, lam, *, nseq, seqlen, row0, hy_prev=None):
    t_all, d = xr.shape
    tt = _pick(seqlen, (256, 128, 64))
    nt = seqlen // tt
    dc = _pick(d, (512, 256, 128))
    nblk = dc // LANES
    rb0 = row0 // tt
    rowmap = lambda s, c, t: (rb0 + s * nt + t, c)
    vec = lambda s, c, t: (0, c)
    in_specs = [
        pl.BlockSpec((tt, dc), rowmap), pl.BlockSpec((tt, dc), rowmap),
        pl.BlockSpec((1, SUBLANES, dc), lambda s, c, t: (s, 0, c)),
        pl.BlockSpec((1, SUBLANES, dc), lambda s, c, t: (s, 0, c)),
        pl.BlockSpec((SUBLANES, dc), vec), pl.BlockSpec((1, dc), vec),
        pl.BlockSpec((nblk, LANES, LANES), lambda s, c, t: (c, 0, 0)), pl.BlockSpec((1, dc), vec),
        pl.BlockSpec((nblk, LANES, LANES), lambda s, c, t: (c, 0, 0)), pl.BlockSpec((1, dc), vec),
        pl.BlockSpec((1, dc), vec),
    ]
    args = [xr, yg, conv0, h0, cw, cb, wa, ba, wx, bx, lam]
    aliases = {}
    if hy_prev is not None:
        in_specs.append(pl.BlockSpec(memory_space=pl.ANY))
        args.append(hy_prev)
        aliases = {len(args) - 1: 0}
    return pl.pallas_call(
        functools.partial(_rglru_kernel, tt=tt, nblk=nblk, n_alias=len(aliases)),
        grid=(nseq, d // dc, nt),
        in_specs=in_specs,
        out_specs=[pl.BlockSpec((tt, dc), rowmap), pl.BlockSpec((1, SUBLANES, dc), lambda s, c, t: (s, 0, c))],
        out_shape=[jax.ShapeDtypeStruct((t_all, d), BF16), jax.ShapeDtypeStruct((nseq, SUBLANES, d), F32)],
        scratch_shapes=[pltpu.VMEM((SUBLANES, dc), F32), pltpu.VMEM((SUBLANES, dc), F32)],
        input_output_aliases=aliases,
        name="rglru",
        compiler_params=_cparams("parallel", "parallel", "arbitrary"),
    )(*args)


def _attn_prompt_kernel(q_ref, k_ref, v_ref, kr_ref, o_ref, *, tq):
    qi = pl.program_id(2)
    q = q_ref[...]

    def step(carry, j0, mask=None):
        m, l, acc = carry
        j0 = pl.multiple_of(j0, tq)
        k = jnp.concatenate([k_ref[pl.ds(j0, tq), :], kr_ref[pl.ds(j0, tq), :].astype(BF16)], axis=1)
        s = _dot_t(q, k)
        if mask is not None:
            s = jnp.where(mask, s, NEG)
        m_new = jnp.maximum(m, jnp.max(s, axis=1, keepdims=True))
        alpha = jnp.exp2(m - m_new)
        p = jnp.exp2(s - m_new)
        l = alpha * l + jnp.sum(p, axis=1, keepdims=True)
        acc = alpha * acc + _dot(p.astype(BF16), v_ref[pl.ds(j0, tq), :])
        return m_new, l, acc

    init = (jnp.full((tq, 1), NEG, F32), jnp.zeros((tq, 1), F32), jnp.zeros((tq, LANES), F32))
    carry = lax.fori_loop(0, qi, lambda j, c: step(c, j * tq), init)
    qc = lax.broadcasted_iota(I32, (tq, tq), 0) // CHUNK
    kc = lax.broadcasted_iota(I32, (tq, tq), 1) // CHUNK
    _, l, acc = step(carry, qi * tq, mask=kc <= qc)
    o_ref[...] = (acc / l).astype(o_ref.dtype)


def attn_prompt(q, kv, krr, *, nbatch, seqlen, nheads):
    t_all = q.shape[0]
    tq = _pick(seqlen, (512, 256, 128, 64))
    nq = seqlen // tq
    return pl.pallas_call(
        functools.partial(_attn_prompt_kernel, tq=tq),
        name="attn_prompt",
        grid=(nbatch, nheads, nq),
        in_specs=[
            pl.BlockSpec((tq, 2 * LANES), lambda b, h, i: (b * nq + i, h)),
            pl.BlockSpec((seqlen, LANES), lambda b, h, i: (b, h)),
            pl.BlockSpec((seqlen, LANES), lambda b, h, i: (b, nheads + h)),
            pl.BlockSpec((seqlen, LANES), lambda b, h, i: (b, 0)),
        ],
        out_specs=pl.BlockSpec((tq, LANES), lambda b, h, i: (b * nq + i, h)),
        out_shape=jax.ShapeDtypeStruct((t_all, nheads * LANES), BF16),
        compiler_params=_cparams("parallel", "parallel", "arbitrary"),
    )(q, kv, kv, krr)


def _attn_sample_kernel(q_ref, cn_ref, kr_ref, past_ref, pkr_ref, wukt_ref, wuv_ref, o_prev, o_ref, *, nheads):
    del o_prev
    q = q_ref[...]
    s_len = q.shape[0]
    lat = jnp.concatenate([past_ref[0].astype(BF16), cn_ref[...].astype(BF16)], axis=0)
    kro = jnp.concatenate([pkr_ref[0].astype(BF16), kr_ref[...].astype(BF16)], axis=0)
    kext = jnp.concatenate([lat, kro], axis=1)
    qs = []
    for h in range(nheads):
        qn = q[:, h * 2 * LANES:h * 2 * LANES + LANES]
        qr = q[:, h * 2 * LANES + LANES:(h + 1) * 2 * LANES]
        qs.append(jnp.concatenate([_dot(qn, wukt_ref[h]).astype(BF16), qr], axis=1))
    qext = jnp.concatenate(qs, axis=0)
    s = _dot_t(qext, kext)
    m = jnp.max(s, axis=1, keepdims=True)
    p = jnp.exp2(s - m)
    l = jnp.sum(p, axis=1, keepdims=True)
    ol = (_dot(p.astype(BF16), lat) / l).astype(BF16)
    outs = [_dot(ol[h * s_len:(h + 1) * s_len], wuv_ref[h]) for h in range(nheads)]
    o_ref[...] = jnp.concatenate(outs, axis=1).astype(o_ref.dtype)


def attn_sample(q, ckv, krr, past_ckv, past_kr2, wukt, wuv, o_prev, *, nbatch, seqlen, row0, nheads):
    t_all = q.shape[0]
    rb0 = row0 // seqlen
    p_len, c = past_ckv.shape[1:]
    return pl.pallas_call(
        functools.partial(_attn_sample_kernel, nheads=nheads),
        grid=(nbatch,),
        in_specs=[
            pl.BlockSpec((seqlen, nheads * 2 * LANES), lambda b: (rb0 + b, 0)),
            pl.BlockSpec((seqlen, c), lambda b: (rb0 + b, 0)),
            pl.BlockSpec((seqlen, LANES), lambda b: (rb0 + b, 0)),
            pl.BlockSpec((1, p_len, c), lambda b: (b, 0, 0)),
            pl.BlockSpec((1, p_len, LANES), lambda b: (b, 0, 0)),
            pl.BlockSpec(wukt.shape, lambda b: (0, 0, 0)),
            pl.BlockSpec(wuv.shape, lambda b: (0, 0, 0)),
            pl.BlockSpec(memory_space=pl.ANY),
        ],
        out_specs=pl.BlockSpec((seqlen, nheads * LANES), lambda b: (rb0 + b, 0)),
        out_shape=jax.ShapeDtypeStruct((t_all, nheads * LANES), BF16),
        input_output_aliases={7: 0},
        name="attn_sample",
        compiler_params=_cparams("parallel"),
    )(q, ckv, krr, past_ckv, past_kr2, wukt, wuv, o_prev)


def _xattn_kernel(x1_ref, gx_ref, wq_ref, mk_ref, mv_ref, wo_ref, gf_ref, rwh_ref, rwl_ref, rb_ref, cnt0_ref,
                  *rest, nb, seg, xheads, n_alias):
    x2_ref, hp_ref, ir_ref, gw_ref, cnt_ref, carry = rest[n_alias:]
    i = pl.program_id(0)

    @pl.when(i == 0)
    def _():
        carry[...] = cnt0_ref[...]

    x1 = x1_ref[...]
    tm = x1.shape[0]
    hn = _rms(x1, gx_ref[...]).astype(BF16)
    q = _dot(hn, wq_ref[...]).astype(BF16)
    scale = LANES ** -0.5
    segs = []
    for n in range(nb):
        heads = []
        for h in range(xheads):
            cols = slice(h * LANES, (h + 1) * LANES)
            qh = q[n * seg:(n + 1) * seg, cols]
            kh = mk_ref[n][:, cols].astype(BF16)
            vh = mv_ref[n][:, cols].astype(BF16)
            s = _dot_t(qh, kh) * scale
            m = jnp.max(s, axis=1, keepdims=True)
            p = jnp.exp(s - m)
            l = jnp.sum(p, axis=1, keepdims=True)
            heads.append(_dot(p.astype(BF16), vh) / l)
        segs.append(jnp.concatenate(heads, axis=1))
    o = (segs[0] if nb == 1 else jnp.concatenate(segs, axis=0)).astype(BF16)
    x2 = x1 + _dot(o, wo_ref[...])
    x2_ref[...] = x2

    hf = _rms(x2, gf_ref[...])
    hp_ref[...] = _pack(hf, hf.shape[1])
    hb = hf.astype(BF16)
    hl = (hf - hb.astype(F32)).astype(BF16)
    logits = _dot(hb, rwh_ref[...]) + _dot(hl, rwh_ref[...]) + _dot(hb, rwl_ref[...]) + rb_ref[...]

    lane = lax.broadcasted_iota(I32, (tm, LANES), 1)
    work = logits
    sels, vals, idxs = [], [], []
    for _ in range(TOP_K):
        m = jnp.max(work, axis=1, keepdims=True)
        idx = jnp.min(jnp.where(work == m, lane, LANES), axis=1, keepdims=True)
        sel = lane == idx
        sels.append(sel)
        vals.append(m)
        idxs.append(idx)
        work = jnp.where(sel, -jnp.inf, work)
    es = [jnp.exp(v - vals[0]) for v in vals]
    den = es[0] + es[1] + es[2] + es[3]
    onehot = (sels[0] | sels[1] | sels[2] | sels[3]).astype(F32)
    tri = (lax.broadcasted_iota(I32, (tm, tm), 0) > lax.broadcasted_iota(I32, (tm, tm), 1)).astype(BF16)
    excl = _dot(tri, onehot.astype(BF16)) + carry[0:1]
    carry[...] = carry[...] + jnp.sum(onehot, axis=0, keepdims=True)
    ir = jnp.zeros((tm, LANES), I32)
    gw = jnp.zeros((tm, LANES), F32)
    for k in range(TOP_K):
        rank = jnp.sum(jnp.where(sels[k], excl, 0.0), axis=1, keepdims=True).astype(I32)
        ir = jnp.where(lane == k, idxs[k], ir)
        ir = jnp.where(lane == TOP_K + k, rank, ir)
        gw = jnp.where(lane == k, es[k] / den, gw)
    ir_ref[...] = ir
    gw_ref[...] = gw
    cnt_ref[...] = carry[...]


def xattn_router(x1, gx, wq, mk, mv, wo, gf, rwh, rwl, rb, cnt0, *, nb, seg, ntiles, row0, xheads, prev=None):
    t_all, d = x1.shape
    tm = nb * seg
    rb0 = row0 // tm
    rows = lambda i: (rb0 + i, 0)
    const = lambda i: (0, 0)
    mem_rows, mem_cols = mk.shape[1:]
    if nb == 1:
        tiles_per_batch = ntiles // mk.shape[0]
        mem_map = lambda i: (i // tiles_per_batch, 0, 0)
    else:
        mem_map = lambda i: (i, 0, 0)
    in_specs = [
        pl.BlockSpec((tm, d), rows), pl.BlockSpec((1, d), const), pl.BlockSpec(wq.shape, const),
        pl.BlockSpec((nb, mem_rows, mem_cols), mem_map), pl.BlockSpec((nb, mem_rows, mem_cols), mem_map),
        pl.BlockSpec(wo.shape, const), pl.BlockSpec((1, d), const),
        pl.BlockSpec(rwh.shape, const), pl.BlockSpec(rwl.shape, const), pl.BlockSpec((1, LANES), const),
        pl.BlockSpec((SUBLANES, LANES), const),
    ]
    args = [x1, gx, wq, mk, mv, wo, gf, rwh, rwl, rb, cnt0]
    aliases = {}
    if prev is not None:
        for k, arr in enumerate(prev):
            in_specs.append(pl.BlockSpec(memory_space=pl.ANY))
            args.append(arr)
            aliases[len(args) - 1] = k
    out_shape = [
        jax.ShapeDtypeStruct((t_all, d), F32), jax.ShapeDtypeStruct((t_all, d // 2), U32),
        jax.ShapeDtypeStruct((t_all, LANES), I32), jax.ShapeDtypeStruct((t_all, LANES), F32),
        jax.ShapeDtypeStruct((SUBLANES, LANES), F32),
    ]
    out_specs = [
        pl.BlockSpec((tm, d), rows), pl.BlockSpec((tm, d // 2), rows),
        pl.BlockSpec((tm, LANES), rows), pl.BlockSpec((tm, LANES), rows),
        pl.BlockSpec((SUBLANES, LANES), const),
    ]
    return pl.pallas_call(
        functools.partial(_xattn_kernel, nb=nb, seg=seg, xheads=xheads, n_alias=len(aliases)),
        grid=(ntiles,),
        in_specs=in_specs,
        out_specs=out_specs,
        out_shape=out_shape,
        scratch_shapes=[pltpu.VMEM((SUBLANES, LANES), F32)],
        input_output_aliases=aliases,
        name="xattn_router",
        compiler_params=_cparams("arbitrary"),
    )(*args)


def _row_copy(src, dst, i, j, sem):
    return pltpu.make_async_copy(src.at[pl.ds(i, 1)], dst.at[pl.ds(j, 1)], sem)


def _token_row_copies(n_tokens, copy, same_size_copy):
    def issue(t, c):
        for k in range(TOP_K):
            copy(t, k).start()
        return c

    lax.fori_loop(0, n_tokens, issue, 0, unroll=DMA_UNROLL)

    def drain(t, c):
        for _ in range(TOP_K):
            same_size_copy.wait()
        return c

    lax.fori_loop(0, n_tokens, drain, 0, unroll=DMA_UNROLL)


def _dispatch_kernel(last_ref, dst_ref, h_ref, xb_out, zbuf, sem, *, tmd, tme, ne):
    @pl.when(pl.program_id(0) == 0)
    def _():
        zbuf[...] = jnp.zeros_like(zbuf)

        def block_copy(e):
            return pltpu.make_async_copy(
                zbuf, xb_out.at[pl.ds(pl.multiple_of(last_ref[e], tme), tme)], sem.at[0])

        def fill(e, c):
            @pl.when(last_ref[e] >= 0)
            def _():
                block_copy(e).start()
            return c

        def drain(e, c):
            @pl.when(last_ref[e] >= 0)
            def _():
                block_copy(e).wait()
            return c

        lax.fori_loop(0, ne, fill, 0)
        lax.fori_loop(0, ne, drain, 0)

    def copy(t, k):
        return _row_copy(h_ref, xb_out, t, dst_ref[k, t], sem.at[0])

    _token_row_copies(tmd, copy, _row_copy(h_ref, xb_out, 0, 0, sem.at[0]))


def moe_dispatch(last, dst_t, hp, n_rows, tme):
    t_all, w = hp.shape
    tmd = _pick(t_all, (512, 256, 128))
    return pl.pallas_call(
        functools.partial(_dispatch_kernel, tmd=tmd, tme=tme, ne=last.shape[0]),
        grid_spec=pltpu.PrefetchScalarGridSpec(
            num_scalar_prefetch=1,
            grid=(t_all // tmd,),
            in_specs=[
                pl.BlockSpec((TOP_K, tmd), lambda i, last: (0, i), memory_space=pltpu.SMEM),
                pl.BlockSpec((tmd, w), lambda i, last: (i, 0)),
            ],
            out_specs=pl.BlockSpec(memory_space=pl.ANY),
            scratch_shapes=[pltpu.VMEM((tme, w), U32), pltpu.SemaphoreType.DMA((1,))],
        ),
        out_shape=jax.ShapeDtypeStruct((n_rows, w), U32),
        name="moe_dispatch",
        compiler_params=_cparams("arbitrary"),
    )(last, dst_t, hp)


def _expert_up_kernel(se, sj, si, sf, sv, xb_ref, wg_ref, wu_ref, bg_ref, bu_ref, o_ref, wgb, wub):
    s = pl.program_id(0)

    @pl.when(sf[s] == 1)
    def _():
        wgb[...] = wg_ref[0].astype(BF16)
        wub[...] = wu_ref[0].astype(BF16)

    @pl.when(sv[s] == 1)
    def _():
        p = xb_ref[...]
        x = _unpack(p, 2 * p.shape[1]).astype(BF16)
        g = jnp.minimum(_dot(x, wgb[...]) + bg_ref[0], SWIGLU_LIMIT)
        u = jnp.clip(_dot(x, wub[...]) + bu_ref[0], -SWIGLU_LIMIT, SWIGLU_LIMIT)
        o_ref[...] = ((u + 1.0) * g * jax.nn.sigmoid(SWIGLU_ALPHA * g)).astype(o_ref.dtype)


def _expert_down_kernel(se, sj, si, sf, sv, a_ref, wd_ref, bd_ref, o_ref, wdb):
    s = pl.program_id(0)

    @pl.when(sf[s] == 1)
    def _():
        wdb[...] = wd_ref[0].astype(BF16)

    @pl.when(sv[s] == 1)
    def _():
        y = _dot(a_ref[...], wdb[...]) + bd_ref[0]
        o_ref[...] = _pack(y, y.shape[1])


def _expert_schedule(nblk_e, nj, nb_max):
    cum = jnp.cumsum(nblk_e)
    start = cum - nblk_e
    total = cum[-1]
    ns = nj * nb_max
    s = jnp.clip(jnp.arange(ns, dtype=I32), 0, jnp.maximum(nj * total - 1, 0))
    e = jnp.sum((s[:, None] >= nj * cum[None, :]).astype(I32), axis=1)
    local = s - nj * start[e]
    n_e = jnp.maximum(nblk_e[e], 1)
    j = local // n_e
    r = local % n_e
    valid = (jnp.arange(ns, dtype=I32) < nj * total).astype(I32)
    first = ((r == 0).astype(I32)) * valid
    return e, j.astype(I32), (start[e] + r).astype(I32), first, valid


def moe_experts(xb, nblk_e, w_gate, b_gate, w_up, b_up, w_down, b_down, *, tme):
    n_rows, half = xb.shape
    d = 2 * half
    ne, _, dff = w_gate.shape
    nb_max = n_rows // tme
    tn_up = _pick(dff, (1024, 512, 256, 128))
    nj = dff // tn_up
    sched = _expert_schedule(nblk_e, nj, nb_max)
    act = pl.pallas_call(
        _expert_up_kernel,
        grid_spec=pltpu.PrefetchScalarGridSpec(
            num_scalar_prefetch=5,
            grid=(nj * nb_max,),
            in_specs=[
                pl.BlockSpec((tme, half), lambda s, se, sj, si, sf, sv: (si[s], 0)),
                pl.BlockSpec((1, d, tn_up), lambda s, se, sj, si, sf, sv: (se[s], 0, sj[s])),
                pl.BlockSpec((1, d, tn_up), lambda s, se, sj, si, sf, sv: (se[s], 0, sj[s])),
                pl.BlockSpec((1, 1, tn_up), lambda s, se, sj, si, sf, sv: (se[s], 0, sj[s])),
                pl.BlockSpec((1, 1, tn_up), lambda s, se, sj, si, sf, sv: (se[s], 0, sj[s])),
            ],
            out_specs=pl.BlockSpec((tme, tn_up), lambda s, se, sj, si, sf, sv: (si[s], sj[s])),
            scratch_shapes=[pltpu.VMEM((d, tn_up), BF16), pltpu.VMEM((d, tn_up), BF16)],
        ),
        out_shape=jax.ShapeDtypeStruct((n_rows, dff), BF16),
        name="expert_up",
        compiler_params=_cparams("arbitrary"),
    )(*sched, xb, w_gate, w_up, b_gate.reshape(ne, 1, dff), b_up.reshape(ne, 1, dff))

    tn_dn = _pick(d, (2048, 1024, 512, 256))
    nj2 = d // tn_dn
    sched2 = _expert_schedule(nblk_e, nj2, nb_max)
    yb = pl.pallas_call(
        _expert_down_kernel,
        grid_spec=pltpu.PrefetchScalarGridSpec(
            num_scalar_prefetch=5,
            grid=(nj2 * nb_max,),
            in_specs=[
                pl.BlockSpec((tme, dff), lambda s, se, sj, si, sf, sv: (si[s], 0)),
                pl.BlockSpec((1, dff, tn_dn), lambda s, se, sj, si, sf, sv: (se[s], 0, sj[s])),
                pl.BlockSpec((1, 1, tn_dn), lambda s, se, sj, si, sf, sv: (se[s], 0, sj[s])),
            ],
            out_specs=pl.BlockSpec((tme, tn_dn // 2), lambda s, se, sj, si, sf, sv: (si[s], sj[s])),
            scratch_shapes=[pltpu.VMEM((dff, tn_dn), BF16)],
        ),
        out_shape=jax.ShapeDtypeStruct((n_rows, d // 2), U32),
        name="expert_down",
        compiler_params=_cparams("arbitrary"),
    )(*sched2, act, w_down, b_down.reshape(ne, 1, d))
    return yb, tn_dn


def _combine_kernel(dst_ref, gw_ref, x2_ref, g_ref, yb_hbm, o_ref, buf, sem, *, tmc, grp):
    def copy(t, k):
        return _row_copy(yb_hbm, buf.at[k], dst_ref[k, t], t, sem.at[0])

    _token_row_copies(tmc, copy, _row_copy(yb_hbm, buf.at[0], 0, 0, sem.at[0]))
    y = x2_ref[...]
    gw = gw_ref[...]
    for k in range(TOP_K):
        y = y + gw[:, k:k + 1] * _unpack(buf[k], grp)
    o_ref[...] = _rms(y, g_ref[...])


def moe_combine(dst_t, gw, x2, g, yb, *, rows, row0, grp):
    d = x2.shape[1]
    tmc = _pick(rows, (256, 128))
    rb0 = row0 // tmc
    return pl.pallas_call(
        functools.partial(_combine_kernel, tmc=tmc, grp=grp),
        grid=(rows // tmc,),
        in_specs=[
            pl.BlockSpec((TOP_K, tmc), lambda i: (0, rb0 + i), memory_space=pltpu.SMEM),
            pl.BlockSpec((tmc, LANES), lambda i: (rb0 + i, 0)),
            pl.BlockSpec((tmc, d), lambda i: (rb0 + i, 0)),
            pl.BlockSpec((1, d), lambda i: (0, 0)),
            pl.BlockSpec(memory_space=pl.ANY),
        ],
        out_specs=pl.BlockSpec((tmc, d), lambda i: (i, 0)),
        scratch_shapes=[pltpu.VMEM((TOP_K, tmc, d // 2), U32), pltpu.SemaphoreType.DMA((1,))],
        out_shape=jax.ShapeDtypeStruct((rows, d), F32),
        name="moe_combine",
        compiler_params=_cparams("arbitrary"),
    )(dst_t, gw, x2, g.reshape(1, d).astype(F32), yb)


def _rot_cols(w):
    half = w.shape[-1] // 2
    return jnp.concatenate([-w[..., half:], w[..., :half]], axis=-1)


def kernel(x_prompt, x_sample, mem_prompt, cache_mla_ckv, cache_mla_krope, cache_mem_k, cache_mem_v, state_rg_h, state_rg_conv, norm_mix_g, w_in, b_in, rg_conv_w, rg_conv_b, rg_a_w, rg_a_b, rg_x_w, rg_x_b, rg_lambda, w_rnn_proj, mla_q_norm_g, mla_w_uq_nope, mla_w_uq_rope, mla_kv_norm_g, mla_w_uk, mla_w_uv, w_mla_proj, w_out, norm_x_g, norm_mem_g, w_xq, w_xk, w_xv, w_xo, norm_ffn_g, router_w, router_b, w_gate, b_gate, w_up, b_up, w_down, b_down, norm_final_g):
    nb_p, s_p, d = x_prompt.shape
    nb_s, s_s, _ = x_sample.shape
    assert w_in.shape[0] == 1, "single-layer step"
    t_p, t_s = nb_p * s_p, nb_s * s_s
    t_all = t_p + t_s
    d_rnn = rg_lambda.shape[-1]
    q_lora, kv_lora = mla_q_norm_g.shape[-1], mla_kv_norm_g.shape[-1]
    nheads, qk_nope = mla_w_uk.shape[2:]
    rope_dim = cache_mla_krope.shape[-1]
    v_dim = mla_w_uv.shape[-1]
    p_len = cache_mla_ckv.shape[2]
    n_mem, xheads, xhd = cache_mem_k.shape[2:]
    ne = router_w.shape[-1]
    assert qk_nope == LANES and v_dim == LANES and xhd == LANES and 2 * rope_dim == LANES
    assert p_len % CHUNK == 0 and s_s <= CHUNK and ne <= LANES

    wi, bi = w_in[0], b_in[0]
    o1, o2 = d_rnn, 2 * d_rnn
    o3, o4 = o2 + q_lora, o2 + q_lora + kv_lora
    o5 = o4 + rope_dim
    o6 = o5 + d
    seg = lambda a, b: (wi[:, a:b].astype(BF16), bi[a:b])
    w_xr, b_xr = seg(0, o1)
    w_yr, b_yr = seg(o1, o2)
    w_cq, b_cq = seg(o2, o3)
    w_ckv, b_ckv = seg(o3, o4)
    w_gr, b_gr = seg(o5, o6)
    w_gm, b_gm = seg(o6, o6 + d)
    w_kr = jnp.concatenate([wi[:, o4:o5], _rot_cols(wi[:, o4:o5])], axis=1).astype(BF16)
    b_kr = jnp.concatenate([bi[o4:o5], _rot_cols(bi[o4:o5])])
    w_q = jnp.concatenate([mla_w_uq_nope[0], mla_w_uq_rope[0], _rot_cols(mla_w_uq_rope[0])], axis=-1)
    w_q = w_q.reshape(q_lora, nheads * 2 * LANES).astype(BF16)
    w_kv = jnp.concatenate([mla_w_uk[0].reshape(kv_lora, -1), mla_w_uv[0].reshape(kv_lora, -1)], axis=1).astype(BF16)
    w_ukt = jnp.transpose(mla_w_uk[0], (1, 2, 0)).astype(BF16)
    w_uvh = jnp.transpose(mla_w_uv[0], (1, 0, 2)).astype(BF16)

    half = rope_dim // 2
    freq = ROPE_BASE ** (-jnp.arange(half, dtype=F32) / half)
    pos = jnp.concatenate([jnp.tile(jnp.arange(s_p, dtype=I32), nb_p),
                           jnp.tile(p_len + jnp.arange(s_s, dtype=I32), nb_s)]).astype(F32)
    ang = pos[:, None] * freq[None, :]
    cs, sn = jnp.cos(ang), jnp.sin(ang)
    ktab = jnp.concatenate([cs, cs, sn, sn], axis=1)
    q_scale = float(qk_nope + rope_dim) ** -0.5 * LOG2_E
    qtab = jnp.concatenate([jnp.ones((t_all, LANES), F32), ktab], axis=1) * q_scale

    x_all = jnp.concatenate([x_prompt.reshape(t_p, d), x_sample.reshape(t_s, d)], axis=0)
    hn = rmsnorm_rows(x_all, norm_mix_g[0], BF16)
    xr = matmul(hn, w_xr, b_xr)
    yg = matmul(hn, w_yr, b_yr, epilogue=jax.nn.gelu, out_dtype=BF16)
    sg_rnn = matmul(hn, w_gr, b_gr, epilogue=jax.nn.sigmoid, out_dtype=BF16)
    sg_mla = matmul(hn, w_gm, b_gm, epilogue=jax.nn.sigmoid, out_dtype=BF16)
    cqn = matmul(hn, w_cq, b_cq, extras=[(mla_q_norm_g[0].reshape(1, -1), "row")], epilogue=_rms,
                 out_dtype=BF16, tn=q_lora)
    ckv = matmul(hn, w_ckv, b_ckv, extras=[(mla_kv_norm_g[0].reshape(1, -1), "row")], epilogue=_rms, tn=kv_lora)

    def rope_epilogue(acc, tab):
        z = acc * tab
        return z + pltpu.roll(z, rope_dim, 1)

    krr = matmul(hn, w_kr, b_kr, extras=[(ktab, "col")], epilogue=rope_epilogue, tn=LANES)

    cw = jnp.pad(rg_conv_w[0], ((0, SUBLANES - CONV_W), (0, 0)))
    cb = rg_conv_b[0].reshape(1, d_rnn)
    wa, wx = rg_a_w[0].astype(BF16), rg_x_w[0].astype(BF16)
    ba, bx = rg_a_b[0].reshape(1, d_rnn), rg_x_b[0].reshape(1, d_rnn)
    lam = rg_lambda[0].reshape(1, d_rnn)
    zeros_state = jnp.zeros((nb_p, SUBLANES, d_rnn), F32)
    hy, hl_p = rglru(xr, yg, zeros_state, zeros_state, cw, cb, wa, ba, wx, bx, lam,
                     nseq=nb_p, seqlen=s_p, row0=0)
    conv_s = jnp.pad(state_rg_conv[0], ((0, 0), (SUBLANES - (CONV_W - 1), 0), (0, 0)))
    h0_s = jnp.broadcast_to(state_rg_h[0][:, None, :], (nb_s, SUBLANES, d_rnn)).astype(F32)
    hy, hl_s = rglru(xr, yg, conv_s, h0_s, cw, cb, wa, ba, wx, bx, lam,
                     nseq=nb_s, seqlen=s_s, row0=t_p, hy_prev=hy)
    m1 = matmul(hy, w_rnn_proj[0].astype(BF16), extras=[(sg_rnn, "tile")],
                epilogue=lambda a, g: a * g.astype(F32), out_dtype=BF16)

    def q_epilogue(acc, tab):
        reps = acc.shape[1] // tab.shape[1]
        return acc * (tab if reps == 1 else jnp.concatenate([tab] * reps, axis=1))

    q = matmul(cqn, w_q, extras=[(qtab, "col")], epilogue=q_epilogue, out_dtype=BF16,
               tn=_pick(nheads * 2 * LANES, (1024, 512, 256)))
    kvp = matmul(ckv, w_kv, out_dtype=BF16, m_rows=t_p)
    o = attn_prompt(q, kvp, krr, nbatch=nb_p, seqlen=s_p, nheads=nheads)
    past_kr2 = jnp.concatenate([cache_mla_krope[0], cache_mla_krope[0]], axis=-1)
    o = attn_sample(q, ckv, krr, cache_mla_ckv[0], past_kr2, w_ukt, w_uvh, o,
                    nbatch=nb_s, seqlen=s_s, row0=t_p, nheads=nheads)
    merged = matmul(o, w_mla_proj[0].astype(BF16), extras=[(sg_mla, "tile"), (m1, "tile")],
                    epilogue=lambda a, g, r: a * g.astype(F32) + r.astype(F32), out_dtype=BF16)
    x1 = matmul(merged, w_out[0].astype(BF16), extras=[(x_all, "tile")], epilogue=lambda a, r: a + r)

    xcols = xheads * xhd
    mn = rmsnorm_rows(mem_prompt.reshape(nb_p * n_mem, d), norm_mem_g[0], BF16)
    mk = matmul(mn, w_xk[0].reshape(d, xcols).astype(BF16))
    mv = matmul(mn, w_xv[0].reshape(d, xcols).astype(BF16))
    rw = jnp.pad(router_w[0], ((0, 0), (0, LANES - ne)))
    rwh = rw.astype(BF16)
    rwl = (rw - rwh.astype(F32)).astype(BF16)
    rb = jnp.pad(router_b[0].astype(F32), (0, LANES - ne), constant_values=NEG).reshape(1, LANES)
    wq_x = w_xq[0].reshape(d, xcols).astype(BF16)
    wo_x = w_xo[0].reshape(xcols, d).astype(BF16)
    gx, gf = norm_x_g[0].reshape(1, d), norm_ffn_g[0].reshape(1, d)
    tile_p = _pick(s_p, (512, 256, 128, 64))
    outs = xattn_router(x1, gx, wq_x, mk.reshape(nb_p, n_mem, xcols), mv.reshape(nb_p, n_mem, xcols), wo_x, gf,
                        rwh, rwl, rb, jnp.zeros((SUBLANES, LANES), F32),
                        nb=1, seg=tile_p, ntiles=t_p // tile_p, row0=0, xheads=xheads)
    nb_x = _pick(nb_s, (8, 4, 2, 1))
    outs = xattn_router(x1, gx, wq_x, cache_mem_k[0].reshape(nb_s, n_mem, xcols),
                        cache_mem_v[0].reshape(nb_s, n_mem, xcols), wo_x, gf, rwh, rwl, rb, outs[4],
                        nb=nb_x, seg=s_s, ntiles=nb_s // nb_x, row0=t_p, xheads=xheads, prev=outs[:4])
    x2, hp, ir, gw, cnt = outs

    tme = 512
    counts = cnt[0, :ne].astype(I32)
    nblk_e = (counts + tme - 1) // tme
    offs = ((jnp.cumsum(nblk_e) - nblk_e) * tme).astype(I32)
    n_rows = (t_all * TOP_K // tme + ne) * tme
    dst_t = jnp.transpose(offs[ir[:, :TOP_K]] + ir[:, TOP_K:2 * TOP_K])
    last = jnp.where(nblk_e > 0, offs + (nblk_e - 1) * tme, -1).astype(I32)
    xb = moe_dispatch(last, dst_t, hp, n_rows, tme)
    yb, grp = moe_experts(xb, nblk_e, w_gate[0], b_gate[0], w_up[0], b_up[0], w_down[0], b_down[0], tme=tme)
    y_p = moe_combine(dst_t, gw, x2, norm_final_g, yb, rows=t_p, row0=0, grp=grp)
    y_s = moe_combine(dst_t, gw, x2, norm_final_g, yb, rows=t_s, row0=t_p, grp=grp)

    nconv = CONV_W - 1
    assert s_p >= nconv and s_s >= nconv
    tail = jnp.arange(-nconv, 0, dtype=I32)[None, :]
    rows_p = (jnp.arange(1, nb_p + 1, dtype=I32) * s_p)[:, None] + tail
    rows_s = t_p + (jnp.arange(1, nb_s + 1, dtype=I32) * s_s)[:, None] + tail
    conv_p = jnp.take(xr, rows_p.reshape(-1), axis=0).reshape(nb_p, nconv, d_rnn)
    conv_new_s = jnp.take(xr, rows_s.reshape(-1), axis=0).reshape(nb_s, nconv, d_rnn)
    return (
        y_p.reshape(nb_p, s_p, d), y_s.reshape(nb_s, s_s, d),
        hl_p[None, :, 0, :], conv_p[None],
        ckv[:t_p].reshape(1, nb_p, s_p, kv_lora), krr[:t_p, :rope_dim].reshape(1, nb_p, s_p, rope_dim),
        mk.reshape(1, nb_p, n_mem, xheads, xhd), mv.reshape(1, nb_p, n_mem, xheads, xhd),
        hl_s[None, :, 0, :], conv_new_s[None],
        ckv[t_p:].reshape(1, nb_s, s_s, kv_lora), krr[t_p:, :rope_dim].reshape(1, nb_s, s_s, rope_dim),
    )
```

```python
import functools

import jax
import jax.numpy as jnp
from jax import lax
from jax.experimental import pallas as pl
from jax.experimental.pallas import tpu as pltpu

F32, BF16, I32, U32 = jnp.float32, jnp.bfloat16, jnp.int32, jnp.uint32

CHUNK = 64
NORM_EPS = 1e-6
RG_C = 8.0
CONV_W = 4
ROPE_BASE = 10000.0
SWIGLU_LIMIT = 7.0
SWIGLU_ALPHA = 1.702
TOP_K = 4
LANES = 128
SUBLANES = 8
VMEM_LIMIT = 56 * 1024 * 1024
DMA_UNROLL = 8
HI_MASK = 0xFFFF0000
NEG = -1e30
LOG2_E = 1.4426950408889634


def _pick(n, prefs):
    for p in prefs:
        if n % p == 0:
            return p
    raise ValueError(f"no tile for {n} in {prefs}")


def _cparams(*sem):
    return pltpu.CompilerParams(dimension_semantics=sem, vmem_limit_bytes=VMEM_LIMIT)


def _rms(x, g):
    return x * lax.rsqrt(jnp.mean(x * x, axis=-1, keepdims=True) + NORM_EPS) * g


def _dot(a, b):
    return jnp.dot(a, b, preferred_element_type=F32)


def _dot_t(a, b):
    return lax.dot_general(a, b, (((1,), (1,)), ((), ())), preferred_element_type=F32)


def _pack(x, grp):
    bits = lax.bitcast_convert_type(x.astype(BF16).astype(F32), U32)
    half = grp // 2
    outs = []
    for g in range(x.shape[1] // grp):
        lo = bits[:, g * grp:g * grp + half]
        hi = bits[:, g * grp + half:(g + 1) * grp]
        outs.append((lo >> 16) | (hi & jnp.uint32(HI_MASK)))
    return outs[0] if len(outs) == 1 else jnp.concatenate(outs, axis=1)


def _unpack(p, grp):
    lo = lax.bitcast_convert_type(p << 16, F32)
    hi = lax.bitcast_convert_type(p & jnp.uint32(HI_MASK), F32)
    half = grp // 2
    outs = []
    for g in range(p.shape[1] // half):
        outs.append(lo[:, g * half:(g + 1) * half])
        outs.append(hi[:, g * half:(g + 1) * half])
    return jnp.concatenate(outs, axis=1)


def _rmsnorm_kernel(x_ref, g_ref, o_ref):
    o_ref[...] = _rms(x_ref[...].astype(F32), g_ref[...]).astype(o_ref.dtype)


def rmsnorm_rows(x, g, out_dtype):
    m, n = x.shape
    tm = _pick(m, (512, 256, 128, 64, 8))
    return pl.pallas_call(
        _rmsnorm_kernel,
        grid=(m // tm,),
        in_specs=[pl.BlockSpec((tm, n), lambda i: (i, 0)), pl.BlockSpec((1, n), lambda i: (0, 0))],
        out_specs=pl.BlockSpec((tm, n), lambda i: (i, 0)),
        out_shape=jax.ShapeDtypeStruct((m, n), out_dtype),
        name="rmsnorm_rows",
        compiler_params=_cparams("parallel"),
    )(x, g.reshape(1, n).astype(F32))


def _mm_kernel(*refs, has_bias, epilogue):
    x_ref, w_ref = refs[0], refs[1]
    acc = _dot(x_ref[...].astype(BF16), w_ref[...])
    k = 2
    if has_bias:
        acc = acc + refs[k][...]
        k += 1
    extras = [r[...] for r in refs[k:-1]]
    o_ref = refs[-1]
    o_ref[...] = epilogue(acc, *extras).astype(o_ref.dtype)


def matmul(x, w, bias=None, extras=(), epilogue=lambda a: a, out_dtype=F32, m_rows=None, tn=None, name="matmul"):
    m = x.shape[0] if m_rows is None else m_rows
    kdim, n = w.shape
    tm = _pick(m, (1024, 512, 256, 128, 64, 8))
    tn = tn or _pick(n, (1024, 512, 256, 128))
    in_specs = [pl.BlockSpec((tm, kdim), lambda i, j: (i, 0)), pl.BlockSpec((kdim, tn), lambda i, j: (0, j))]
    args = [x, w]
    if bias is not None:
        in_specs.append(pl.BlockSpec((1, tn), lambda i, j: (0, j)))
        args.append(bias.reshape(1, n).astype(F32))
    for arr, kind in extras:
        if kind == "tile":
            in_specs.append(pl.BlockSpec((tm, tn), lambda i, j: (i, j)))
        elif kind == "row":
            in_specs.append(pl.BlockSpec((1, tn), lambda i, j: (0, j)))
        else:
            in_specs.append(pl.BlockSpec((tm, arr.shape[1]), lambda i, j: (i, 0)))
        args.append(arr)
    return pl.pallas_call(
        functools.partial(_mm_kernel, has_bias=bias is not None, epilogue=epilogue),
        grid=(m // tm, n // tn),
        in_specs=in_specs,
        out_specs=pl.BlockSpec((tm, tn), lambda i, j: (i, j)),
        out_shape=jax.ShapeDtypeStruct((m, n), out_dtype),
        name=name,
        compiler_params=_cparams("parallel", "parallel"),
    )(*args)


def _rglru_kernel(xr_ref, yg_ref, conv0_ref, h0_ref, cw_ref, cb_ref, wa_ref, ba_ref, wx_ref, bx_ref,
                  lam_ref, *rest, tt, nblk, n_alias):
    hy_ref, hl_ref, cx, ch = rest[n_alias:]
    t = pl.program_id(2)

    @pl.when(t == 0)
    def _():
        cx[0:SUBLANES, :] = conv0_ref[0]
        ch[...] = h0_ref[0]

    cx[SUBLANES:, :] = xr_ref[...]
    cw = cw_ref[...]
    xc = cb_ref[...]
    for k in range(CONV_W):
        off = SUBLANES - (CONV_W - 1) + k
        xc = xc + cx[off:off + tt, :] * cw[k:k + 1]
    cx[0:SUBLANES, :] = cx[tt:tt + SUBLANES, :]

    ra, ia = [], []
    for b in range(nblk):
        xb = xc[:, b * LANES:(b + 1) * LANES].astype(BF16)
        ra.append(_dot(xb, wa_ref[b]))
        ia.append(_dot(xb, wx_ref[b]))
    cat = (lambda v: v[0] if len(v) == 1 else jnp.concatenate(v, axis=1))
    r = jax.nn.sigmoid(cat(ra) + ba_ref[...])
    gi = jax.nn.sigmoid(cat(ia) + bx_ref[...])
    log_a = -RG_C * r * jax.nn.softplus(-lam_ref[...])
    a = jnp.exp(log_a)
    u = jnp.sqrt(-jnp.tanh(log_a) * (1.0 + a * a)) * (gi * xc)

    ngrp, dc = tt // SUBLANES, a.shape[1]
    a = a.reshape(ngrp, SUBLANES, dc)
    u = u.reshape(ngrp, SUBLANES, dc)
    sub = lax.broadcasted_iota(I32, (1, SUBLANES, 1), 1)
    d = 1
    while d < SUBLANES:
        keep = sub >= d
        a_s = jnp.where(keep, pltpu.roll(a, d, 1), 1.0)
        u_s = jnp.where(keep, pltpu.roll(u, d, 1), 0.0)
        u = a * u_s + u
        a = a * a_s
        d *= 2
    last = ch[0:1]
    groups = []
    for g in range(ngrp):
        hg = u[g] + a[g] * last
        groups.append(hg)
        last = hg[SUBLANES - 1:SUBLANES]
    h = jnp.concatenate(groups, axis=0)
    ch[...] = jnp.broadcast_to(last, ch.shape)
    hy_ref[...] = (h * yg_ref[...].astype(F32)).astype(hy_ref.dtype)
    hl_ref[0] = jnp.broadcast_to(last, hl_ref.shape[1:])


def rglru(xr, yg, conv0, h0, cw, cb, wa, ba, wx, bx, lam, *, nseq, seqlen, row0, hy_prev=None):
    t_all, d = xr.shape
    tt = _pick(seqlen, (256, 128, 64))
    nt = seqlen // tt
    dc = _pick(d, (512, 256, 128))
    nblk = dc // LANES
    rb0 = row0 // tt
    rowmap = lambda s, c, t: (rb0 + s * nt + t, c)
    vec = lambda s, c, t: (0, c)
    in_specs = [
        pl.BlockSpec((tt, dc), rowmap), pl.BlockSpec((tt, dc), rowmap),
        pl.BlockSpec((1, SUBLANES, dc), lambda s, c, t: (s, 0, c)),
        pl.BlockSpec((1, SUBLANES, dc), lambda s, c, t: (s, 0, c)),
        pl.BlockSpec((SUBLANES, dc), vec), pl.BlockSpec((1, dc), vec),
        pl.BlockSpec((nblk, LANES, LANES), lambda s, c, t: (c, 0, 0)), pl.BlockSpec((1, dc), vec),
        pl.BlockSpec((nblk, LANES, LANES), lambda s, c, t: (c, 0, 0)), pl.BlockSpec((1, dc), vec),
        pl.BlockSpec((1, dc), vec),
    ]
    args = [xr, yg, conv0, h0, cw, cb, wa, ba, wx, bx, lam]
    aliases = {}
    if hy_prev is not None:
        in_specs.append(pl.BlockSpec(memory_space=pl.ANY))
        args.append(hy_prev)
        aliases = {len(args) - 1: 0}
    return pl.pallas_call(
        functools.partial(_rglru_kernel, tt=tt, nblk=nblk, n_alias=len(aliases)),
        grid=(nseq, d // dc, nt),
        in_specs=in_specs,
        out_specs=[pl.BlockSpec((tt, dc), rowmap), pl.BlockSpec((1, SUBLANES, dc), lambda s, c, t: (s, 0, c))],
        out_shape=[jax.ShapeDtypeStruct((t_all, d), BF16), jax.ShapeDtypeStruct((nseq, SUBLANES, d), F32)],
        scratch_shapes=[pltpu.VMEM((tt + SUBLANES, dc), F32), pltpu.VMEM((SUBLANES, dc), F32)],
        input_output_aliases=aliases,
        name="rglru",
        compiler_params=_cparams("parallel", "parallel", "arbitrary"),
    )(*args)


def _attn_prompt_kernel(q_ref, k_ref, v_ref, kr_ref, o_ref, *, tq):
    qi = pl.program_id(2)
    q = q_ref[...]

    def step(carry, j0, n, mask=None):
        m, l, acc = carry
        j0 = pl.multiple_of(j0, tq)
        k = jnp.concatenate([k_ref[pl.ds(j0, n), :], kr_ref[pl.ds(j0, n), :].astype(BF16)], axis=1)
        s = _dot_t(q, k)
        if mask is not None:
            s = jnp.where(mask, s, NEG)
        m_new = jnp.maximum(m, jnp.max(s, axis=1, keepdims=True))
        alpha = jnp.exp2(m - m_new)
        p = jnp.exp2(s - m_new)
        l = alpha * l + jnp.sum(p, axis=1, keepdims=True)
        acc = alpha * acc + _dot(p.astype(BF16), v_ref[pl.ds(j0, n), :])
        return m_new, l, acc

    init = (jnp.full((tq, 1), NEG, F32), jnp.zeros((tq, 1), F32), jnp.zeros((tq, LANES), F32))
    carry = lax.fori_loop(0, qi // 2, lambda j, c: step(c, j * 2 * tq, 2 * tq), init)
    def visible(n):
        qc = lax.broadcasted_iota(I32, (tq, n), 0) // CHUNK
        kc = lax.broadcasted_iota(I32, (tq, n), 1) // CHUNK
        return kc - (n - tq) // CHUNK <= qc

    def tail_with_earlier_block(c):
        return step(c, (qi - 1) * tq, 2 * tq, mask=visible(2 * tq))

    def tail_diagonal_only(c):
        return step(c, qi * tq, tq, mask=visible(tq))

    _, l, acc = lax.cond(qi % 2 == 1, tail_with_earlier_block, tail_diagonal_only, carry)
    o_ref[...] = (acc / l).astype(o_ref.dtype)


def attn_prompt(q, kv, krr, *, nbatch, seqlen, nheads):
    t_all = q.shape[0]
    tq = _pick(seqlen, (512, 256, 128, 64))
    nq = seqlen // tq
    return pl.pallas_call(
        functools.partial(_attn_prompt_kernel, tq=tq),
        name="attn_prompt",
        grid=(nbatch, nheads, nq),
        in_specs=[
            pl.BlockSpec((tq, 2 * LANES), lambda b, h, i: (b * nq + i, h)),
            pl.BlockSpec((seqlen, LANES), lambda b, h, i: (b, h)),
            pl.BlockSpec((seqlen, LANES), lambda b, h, i: (b, nheads + h)),
            pl.BlockSpec((seqlen, LANES), lambda b, h, i: (b, 0)),
        ],
        out_specs=pl.BlockSpec((tq, LANES), lambda b, h, i: (b * nq + i, h)),
        out_shape=jax.ShapeDtypeStruct((t_all, nheads * LANES), BF16),
        compiler_params=_cparams("parallel", "parallel", "arbitrary"),
    )(q, kv, kv, krr)


def _attn_sample_kernel(q_ref, cn_ref, kr_ref, past_ref, pkr_ref, wukt_ref, wuv_ref, o_prev, o_ref, *, nheads):
    del o_prev
    q = q_ref[...]
    s_len = q.shape[0]
    lat = jnp.concatenate([past_ref[0].astype(BF16), cn_ref[...].astype(BF16)], axis=0)
    kro = jnp.concatenate([pkr_ref[0].astype(BF16), kr_ref[...].astype(BF16)], axis=0)
    kext = jnp.concatenate([lat, kro], axis=1)
    qs = []
    for h in range(nheads):
        qn = q[:, h * 2 * LANES:h * 2 * LANES + LANES]
        qr = q[:, h * 2 * LANES + LANES:(h + 1) * 2 * LANES]
        qs.append(jnp.concatenate([_dot(qn, wukt_ref[h]).astype(BF16), qr], axis=1))
    qext = jnp.concatenate(qs, axis=0)
    s = _dot_t(qext, kext)
    m = jnp.max(s, axis=1, keepdims=True)
    p = jnp.exp2(s - m)
    l = jnp.sum(p, axis=1, keepdims=True)
    ol = (_dot(p.astype(BF16), lat) / l).astype(BF16)
    outs = [_dot(ol[h * s_len:(h + 1) * s_len], wuv_ref[h]) for h in range(nheads)]
    o_ref[...] = jnp.concatenate(outs, axis=1).astype(o_ref.dtype)


def attn_sample(q, ckv, krr, past_ckv, past_kr2, wukt, wuv, o_prev, *, nbatch, seqlen, row0, nheads):
    t_all = q.shape[0]
    rb0 = row0 // seqlen
    p_len, c = past_ckv.shape[1:]
    return pl.pallas_call(
        functools.partial(_attn_sample_kernel, nheads=nheads),
        grid=(nbatch,),
        in_specs=[
            pl.BlockSpec((seqlen, nheads * 2 * LANES), lambda b: (rb0 + b, 0)),
            pl.BlockSpec((seqlen, c), lambda b: (rb0 + b, 0)),
            pl.BlockSpec((seqlen, LANES), lambda b: (rb0 + b, 0)),
            pl.BlockSpec((1, p_len, c), lambda b: (b, 0, 0)),
            pl.BlockSpec((1, p_len, LANES), lambda b: (b, 0, 0)),
            pl.BlockSpec(wukt.shape, lambda b: (0, 0, 0)),
            pl.BlockSpec(wuv.shape, lambda b: (0, 0, 0)),
            pl.BlockSpec(memory_space=pl.ANY),
        ],
        out_specs=pl.BlockSpec((seqlen, nheads * LANES), lambda b: (rb0 + b, 0)),
        out_shape=jax.ShapeDtypeStruct((t_all, nheads * LANES), BF16),
        input_output_aliases={7: 0},
        name="attn_sample",
        compiler_params=_cparams("parallel"),
    )(q, ckv, krr, past_ckv, past_kr2, wukt, wuv, o_prev)


def _xattn_kernel(x1_ref, gx_ref, wq_ref, mk_ref, mv_ref, wo_ref, gf_ref, rwh_ref, rwl_ref, rb_ref, cnt0_ref,
                  *rest, nb, seg, xheads, n_alias):
    x2_ref, hp_ref, ir_ref, gw_ref, cnt_ref, carry = rest[n_alias:]
    i = pl.program_id(0)

    @pl.when(i == 0)
    def _():
        carry[...] = cnt0_ref[...]

    x1 = x1_ref[...]
    tm = x1.shape[0]
    hn = _rms(x1, gx_ref[...]).astype(BF16)
    q = _dot(hn, wq_ref[...]).astype(BF16)
    scale = LANES ** -0.5
    segs = []
    for n in range(nb):
        heads = []
        for h in range(xheads):
            cols = slice(h * LANES, (h + 1) * LANES)
            qh = q[n * seg:(n + 1) * seg, cols]
            kh = mk_ref[n][:, cols].astype(BF16)
            vh = mv_ref[n][:, cols].astype(BF16)
            s = _dot_t(qh, kh) * scale
            m = jnp.max(s, axis=1, keepdims=True)
            p = jnp.exp(s - m)
            l = jnp.sum(p, axis=1, keepdims=True)
            heads.append(_dot(p.astype(BF16), vh) / l)
        segs.append(jnp.concatenate(heads, axis=1))
    o = (segs[0] if nb == 1 else jnp.concatenate(segs, axis=0)).astype(BF16)
    x2 = x1 + _dot(o, wo_ref[...])
    x2_ref[...] = x2

    hf = _rms(x2, gf_ref[...])
    hp_ref[...] = _pack(hf, hf.shape[1])
    hb = hf.astype(BF16)
    hl = (hf - hb.astype(F32)).astype(BF16)
    logits = _dot(hb, rwh_ref[...]) + _dot(hl, rwh_ref[...]) + _dot(hb, rwl_ref[...]) + rb_ref[...]

    lane = lax.broadcasted_iota(I32, (tm, LANES), 1)
    work = logits
    sels, vals, idxs = [], [], []
    for _ in range(TOP_K):
        m = jnp.max(work, axis=1, keepdims=True)
        idx = jnp.min(jnp.where(work == m, lane, LANES), axis=1, keepdims=True)
        sel = lane == idx
        sels.append(sel)
        vals.append(m)
        idxs.append(idx)
        work = jnp.where(sel, -jnp.inf, work)
    es = [jnp.exp(v - vals[0]) for v in vals]
    den = es[0] + es[1] + es[2] + es[3]
    onehot = (sels[0] | sels[1] | sels[2] | sels[3]).astype(F32)
    tri = (lax.broadcasted_iota(I32, (tm, tm), 0) > lax.broadcasted_iota(I32, (tm, tm), 1)).astype(BF16)
    excl = _dot(tri, onehot.astype(BF16)) + carry[0:1]
    carry[...] = carry[...] + jnp.sum(onehot, axis=0, keepdims=True)
    ir = jnp.zeros((tm, LANES), I32)
    gw = jnp.zeros((tm, LANES), F32)
    for k in range(TOP_K):
        rank = jnp.sum(jnp.where(sels[k], excl, 0.0), axis=1, keepdims=True).astype(I32)
        ir = jnp.where(lane == k, idxs[k], ir)
        ir = jnp.where(lane == TOP_K + k, rank, ir)
        gw = jnp.where(lane == k, es[k] / den, gw)
    ir_ref[...] = ir
    gw_ref[...] = gw
    cnt_ref[...] = carry[...]


def xattn_router(x1, gx, wq, mk, mv, wo, gf, rwh, rwl, rb, cnt0, *, nb, seg, ntiles, row0, xheads, prev=None):
    t_all, d = x1.shape
    tm = nb * seg
    rb0 = row0 // tm
    rows = lambda i: (rb0 + i, 0)
    const = lambda i: (0, 0)
    mem_rows, mem_cols = mk.shape[1:]
    if nb == 1:
        tiles_per_batch = ntiles // mk.shape[0]
        mem_map = lambda i: (i // tiles_per_batch, 0, 0)
    else:
        mem_map = lambda i: (i, 0, 0)
    in_specs = [
        pl.BlockSpec((tm, d), rows), pl.BlockSpec((1, d), const), pl.BlockSpec(wq.shape, const),
        pl.BlockSpec((nb, mem_rows, mem_cols), mem_map), pl.BlockSpec((nb, mem_rows, mem_cols), mem_map),
        pl.BlockSpec(wo.shape, const), pl.BlockSpec((1, d), const),
        pl.BlockSpec(rwh.shape, const), pl.BlockSpec(rwl.shape, const), pl.BlockSpec((1, LANES), const),
        pl.BlockSpec((SUBLANES, LANES), const),
    ]
    args = [x1, gx, wq, mk, mv, wo, gf, rwh, rwl, rb, cnt0]
    aliases = {}
    if prev is not None:
        for k, arr in enumerate(prev):
            in_specs.append(pl.BlockSpec(memory_space=pl.ANY))
            args.append(arr)
            aliases[len(args) - 1] = k
    out_shape = [
        jax.ShapeDtypeStruct((t_all, d), F32), jax.ShapeDtypeStruct((t_all, d // 2), U32),
        jax.ShapeDtypeStruct((t_all, LANES), I32), jax.ShapeDtypeStruct((t_all, LANES), F32),
        jax.ShapeDtypeStruct((SUBLANES, LANES), F32),
    ]
    out_specs = [
        pl.BlockSpec((tm, d), rows), pl.BlockSpec((tm, d // 2), rows),
        pl.BlockSpec((tm, LANES), rows), pl.BlockSpec((tm, LANES), rows),
        pl.BlockSpec((SUBLANES, LANES), const),
    ]
    return pl.pallas_call(
        functools.partial(_xattn_kernel, nb=nb, seg=seg, xheads=xheads, n_alias=len(aliases)),
        grid=(ntiles,),
        in_specs=in_specs,
        out_specs=out_specs,
        out_shape=out_shape,
        scratch_shapes=[pltpu.VMEM((SUBLANES, LANES), F32)],
        input_output_aliases=aliases,
        name="xattn_router",
        compiler_params=_cparams("arbitrary"),
    )(*args)


def _row_copy(src, dst, i, j, sem):
    return pltpu.make_async_copy(src.at[pl.ds(i, 1)], dst.at[pl.ds(j, 1)], sem)


def _token_row_copies(n_tokens, copy, same_size_copy):
    def issue(t, c):
        for k in range(TOP_K):
            copy(t, k).start()
        return c

    lax.fori_loop(0, n_tokens, issue, 0, unroll=DMA_UNROLL)

    def drain(t, c):
        for _ in range(TOP_K):
            same_size_copy.wait()
        return c

    lax.fori_loop(0, n_tokens, drain, 0, unroll=DMA_UNROLL)


def _dispatch_kernel(last_ref, dst_ref, h_ref, xb_out, zbuf, sem, *, tmd, tme, ne):
    @pl.when(pl.program_id(0) == 0)
    def _():
        zbuf[...] = jnp.zeros_like(zbuf)

        def block_copy(e):
            return pltpu.make_async_copy(
                zbuf, xb_out.at[pl.ds(pl.multiple_of(last_ref[e], tme), tme)], sem.at[0])

        def fill(e, c):
            @pl.when(last_ref[e] >= 0)
            def _():
                block_copy(e).start()
            return c

        def drain(e, c):
            @pl.when(last_ref[e] >= 0)
            def _():
                block_copy(e).wait()
            return c

        lax.fori_loop(0, ne, fill, 0)
        lax.fori_loop(0, ne, drain, 0)

    def copy(t, k):
        return _row_copy(h_ref, xb_out, t, dst_ref[k, t], sem.at[0])

    _token_row_copies(tmd, copy, _row_copy(h_ref, xb_out, 0, 0, sem.at[0]))


def moe_dispatch(last, dst_t, hp, n_rows, tme):
    t_all, w = hp.shape
    tmd = _pick(t_all, (512, 256, 128))
    return pl.pallas_call(
        functools.partial(_dispatch_kernel, tmd=tmd, tme=tme, ne=last.shape[0]),
        grid_spec=pltpu.PrefetchScalarGridSpec(
            num_scalar_prefetch=1,
            grid=(t_all // tmd,),
            in_specs=[
                pl.BlockSpec((TOP_K, tmd), lambda i, last: (0, i), memory_space=pltpu.SMEM),
                pl.BlockSpec((tmd, w), lambda i, last: (i, 0)),
            ],
            out_specs=pl.BlockSpec(memory_space=pl.ANY),
            scratch_shapes=[pltpu.VMEM((tme, w), U32), pltpu.SemaphoreType.DMA((1,))],
        ),
        out_shape=jax.ShapeDtypeStruct((n_rows, w), U32),
        name="moe_dispatch",
        compiler_params=_cparams("arbitrary"),
    )(last, dst_t, hp)


def _expert_up_kernel(se, sj, si, sf, sv, xb_ref, wg_ref, wu_ref, bg_ref, bu_ref, o_ref, wgb, wub):
    s = pl.program_id(0)

    @pl.when(sf[s] == 1)
    def _():
        wgb[...] = wg_ref[0].astype(BF16)
        wub[...] = wu_ref[0].astype(BF16)

    @pl.when(sv[s] == 1)
    def _():
        p = xb_ref[...]
        x = _unpack(p, 2 * p.shape[1]).astype(BF16)
        g = jnp.minimum(_dot(x, wgb[...]) + bg_ref[0], SWIGLU_LIMIT)
        u = jnp.clip(_dot(x, wub[...]) + bu_ref[0], -SWIGLU_LIMIT, SWIGLU_LIMIT)
        o_ref[...] = ((u + 1.0) * g * jax.nn.sigmoid(SWIGLU_ALPHA * g)).astype(o_ref.dtype)


def _expert_down_kernel(se, sj, si, sf, sv, a_ref, wd_ref, bd_ref, o_ref, wdb):
    s = pl.program_id(0)

    @pl.when(sf[s] == 1)
    def _():
        wdb[...] = wd_ref[0].astype(BF16)

    @pl.when(sv[s] == 1)
    def _():
        y = _dot(a_ref[...], wdb[...]) + bd_ref[0]
        o_ref[...] = _pack(y, y.shape[1])


def _expert_schedule(nblk_e, nj, nb_max):
    cum = jnp.cumsum(nblk_e)
    start = cum - nblk_e
    total = cum[-1]
    ns = nj * nb_max
    s = jnp.clip(jnp.arange(ns, dtype=I32), 0, jnp.maximum(nj * total - 1, 0))
    e = jnp.sum((s[:, None] >= nj * cum[None, :]).astype(I32), axis=1)
    local = s - nj * start[e]
    n_e = jnp.maximum(nblk_e[e], 1)
    j = local // n_e
    r = local % n_e
    valid = (jnp.arange(ns, dtype=I32) < nj * total).astype(I32)
    first = ((r == 0).astype(I32)) * valid
    return e, j.astype(I32), (start[e] + r).astype(I32), first, valid


def moe_experts(xb, nblk_e, w_gate, b_gate, w_up, b_up, w_down, b_down, *, tme):
    n_rows, half = xb.shape
    d = 2 * half
    ne, _, dff = w_gate.shape
    nb_max = n_rows // tme
    tn_up = _pick(dff, (1024, 512, 256, 128))
    nj = dff // tn_up
    sched = _expert_schedule(nblk_e, nj, nb_max)
    act = pl.pallas_call(
        _expert_up_kernel,
        grid_spec=pltpu.PrefetchScalarGridSpec(
            num_scalar_prefetch=5,
            grid=(nj * nb_max,),
            in_specs=[
                pl.BlockSpec((tme, half), lambda s, se, sj, si, sf, sv: (si[s], 0)),
                pl.BlockSpec((1, d, tn_up), lambda s, se, sj, si, sf, sv: (se[s], 0, sj[s])),
                pl.BlockSpec((1, d, tn_up), lambda s, se, sj, si, sf, sv: (se[s], 0, sj[s])),
                pl.BlockSpec((1, 1, tn_up), lambda s, se, sj, si, sf, sv: (se[s], 0, sj[s])),
                pl.BlockSpec((1, 1, tn_up), lambda s, se, sj, si, sf, sv: (se[s], 0, sj[s])),
            ],
            out_specs=pl.BlockSpec((tme, tn_up), lambda s, se, sj, si, sf, sv: (si[s], sj[s])),
            scratch_shapes=[pltpu.VMEM((d, tn_up), BF16), pltpu.VMEM((d, tn_up), BF16)],
        ),
        out_shape=jax.ShapeDtypeStruct((n_rows, dff), BF16),
        name="expert_up",
        compiler_params=_cparams("arbitrary"),
    )(*sched, xb, w_gate, w_up, b_gate.reshape(ne, 1, dff), b_up.reshape(ne, 1, dff))

    tn_dn = _pick(d, (2048, 1024, 512, 256))
    nj2 = d // tn_dn
    sched2 = _expert_schedule(nblk_e, nj2, nb_max)
    yb = pl.pallas_call(
        _expert_down_kernel,
        grid_spec=pltpu.PrefetchScalarGridSpec(
            num_scalar_prefetch=5,
            grid=(nj2 * nb_max,),
            in_specs=[
                pl.BlockSpec((tme, dff), lambda s, se, sj, si, sf, sv: (si[s], 0)),
                pl.BlockSpec((1, dff, tn_dn), lambda s, se, sj, si, sf, sv: (se[s], 0, sj[s])),
                pl.BlockSpec((1, 1, tn_dn), lambda s, se, sj, si, sf, sv: (se[s], 0, sj[s])),
            ],
            out_specs=pl.BlockSpec((tme, tn_dn // 2), lambda s, se, sj, si, sf, sv: (si[s], sj[s])),
            scratch_shapes=[pltpu.VMEM((dff, tn_dn), BF16)],
        ),
        out_shape=jax.ShapeDtypeStruct((n_rows, d // 2), U32),
        name="expert_down",
        compiler_params=_cparams("arbitrary"),
    )(*sched2, act, w_down, b_down.reshape(ne, 1, d))
    return yb, tn_dn


def _combine_kernel(dst_ref, gw_ref, x2_ref, g_ref, yb_hbm, o_ref, buf, sem, *, tmc, grp):
    def copy(t, k):
        return _row_copy(yb_hbm, buf.at[k], dst_ref[k, t], t, sem.at[0])

    _token_row_copies(tmc, copy, _row_copy(yb_hbm, buf.at[0], 0, 0, sem.at[0]))
    y = x2_ref[...]
    gw = gw_ref[...]
    for k in range(TOP_K):
        y = y + gw[:, k:k + 1] * _unpack(buf[k], grp)
    o_ref[...] = _rms(y, g_ref[...])


def moe_combine(dst_t, gw, x2, g, yb, *, rows, row0, grp):
    d = x2.shape[1]
    tmc = _pick(rows, (256, 128))
    rb0 = row0 // tmc
    return pl.pallas_call(
        functools.partial(_combine_kernel, tmc=tmc, grp=grp),
        grid=(rows // tmc,),
        in_specs=[
            pl.BlockSpec((TOP_K, tmc), lambda i: (0, rb0 + i), memory_space=pltpu.SMEM),
            pl.BlockSpec((tmc, LANES), lambda i: (rb0 + i, 0)),
            pl.BlockSpec((tmc, d), lambda i: (rb0 + i, 0)),
            pl.BlockSpec((1, d), lambda i: (0, 0)),
            pl.BlockSpec(memory_space=pl.ANY),
        ],
        out_specs=pl.BlockSpec((tmc, d), lambda i: (i, 0)),
        scratch_shapes=[pltpu.VMEM((TOP_K, tmc, d // 2), U32), pltpu.SemaphoreType.DMA((1,))],
        out_shape=jax.ShapeDtypeStruct((rows, d), F32),
        name="moe_combine",
        compiler_params=_cparams("arbitrary"),
    )(dst_t, gw, x2, g.reshape(1, d).astype(F32), yb)


def _rot_cols(w):
    half = w.shape[-1] // 2
    return jnp.concatenate([-w[..., half:], w[..., :half]], axis=-1)


def kernel(x_prompt, x_sample, mem_prompt, cache_mla_ckv, cache_mla_krope, cache_mem_k, cache_mem_v, state_rg_h, state_rg_conv, norm_mix_g, w_in, b_in, rg_conv_w, rg_conv_b, rg_a_w, rg_a_b, rg_x_w, rg_x_b, rg_lambda, w_rnn_proj, mla_q_norm_g, mla_w_uq_nope, mla_w_uq_rope, mla_kv_norm_g, mla_w_uk, mla_w_uv, w_mla_proj, w_out, norm_x_g, norm_mem_g, w_xq, w_xk, w_xv, w_xo, norm_ffn_g, router_w, router_b, w_gate, b_gate, w_up, b_up, w_down, b_down, norm_final_g):
    nb_p, s_p, d = x_prompt.shape
    nb_s, s_s, _ = x_sample.shape
    assert w_in.shape[0] == 1, "single-layer step"
    t_p, t_s = nb_p * s_p, nb_s * s_s
    t_all = t_p + t_s
    d_rnn = rg_lambda.shape[-1]
    q_lora, kv_lora = mla_q_norm_g.shape[-1], mla_kv_norm_g.shape[-1]
    nheads, qk_nope = mla_w_uk.shape[2:]
    rope_dim = cache_mla_krope.shape[-1]
    v_dim = mla_w_uv.shape[-1]
    p_len = cache_mla_ckv.shape[2]
    n_mem, xheads, xhd = cache_mem_k.shape[2:]
    ne = router_w.shape[-1]
    assert qk_nope == LANES and v_dim == LANES and xhd == LANES and 2 * rope_dim == LANES
    assert p_len % CHUNK == 0 and s_s <= CHUNK and ne <= LANES

    wi, bi = w_in[0], b_in[0]
    o1, o2 = d_rnn, 2 * d_rnn
    o3, o4 = o2 + q_lora, o2 + q_lora + kv_lora
    o5 = o4 + rope_dim
    o6 = o5 + d
    seg = lambda a, b: (wi[:, a:b].astype(BF16), bi[a:b])
    w_xr, b_xr = seg(0, o1)
    w_yr, b_yr = seg(o1, o2)
    w_cq, b_cq = seg(o2, o3)
    w_ckv, b_ckv = seg(o3, o4)
    w_gr, b_gr = seg(o5, o6)
    w_gm, b_gm = seg(o6, o6 + d)
    w_kr = jnp.concatenate([wi[:, o4:o5], _rot_cols(wi[:, o4:o5])], axis=1).astype(BF16)
    b_kr = jnp.concatenate([bi[o4:o5], _rot_cols(bi[o4:o5])])
    w_q = jnp.concatenate([mla_w_uq_nope[0], mla_w_uq_rope[0], _rot_cols(mla_w_uq_rope[0])], axis=-1)
    w_q = w_q.reshape(q_lora, nheads * 2 * LANES).astype(BF16)
    w_kv = jnp.concatenate([mla_w_uk[0].reshape(kv_lora, -1), mla_w_uv[0].reshape(kv_lora, -1)], axis=1).astype(BF16)
    w_ukt = jnp.transpose(mla_w_uk[0], (1, 2, 0)).astype(BF16)
    w_uvh = jnp.transpose(mla_w_uv[0], (1, 0, 2)).astype(BF16)

    half = rope_dim // 2
    freq = ROPE_BASE ** (-jnp.arange(half, dtype=F32) / half)
    pos = jnp.concatenate([jnp.tile(jnp.arange(s_p, dtype=I32), nb_p),
                           jnp.tile(p_len + jnp.arange(s_s, dtype=I32), nb_s)]).astype(F32)
    ang = pos[:, None] * freq[None, :]
    cs, sn = jnp.cos(ang), jnp.sin(ang)
    ktab = jnp.concatenate([cs, cs, sn, sn], axis=1)
    q_scale = float(qk_nope + rope_dim) ** -0.5 * LOG2_E
    qtab = jnp.concatenate([jnp.ones((t_all, LANES), F32), ktab], axis=1) * q_scale

    x_all = jnp.concatenate([x_prompt.reshape(t_p, d), x_sample.reshape(t_s, d)], axis=0)
    hn = rmsnorm_rows(x_all, norm_mix_g[0], BF16)
    xr = matmul(hn, w_xr, b_xr)
    yg = matmul(hn, w_yr, b_yr, epilogue=jax.nn.gelu, out_dtype=BF16)
    sg_rnn = matmul(hn, w_gr, b_gr, epilogue=jax.nn.sigmoid, out_dtype=BF16)
    sg_mla = matmul(hn, w_gm, b_gm, epilogue=jax.nn.sigmoid, out_dtype=BF16)
    cqn = matmul(hn, w_cq, b_cq, extras=[(mla_q_norm_g[0].reshape(1, -1), "row")], epilogue=_rms,
                 out_dtype=BF16, tn=q_lora)
    ckv = matmul(hn, w_ckv, b_ckv, extras=[(mla_kv_norm_g[0].reshape(1, -1), "row")], epilogue=_rms, tn=kv_lora)

    def rope_epilogue(acc, tab):
        z = acc * tab
        return z + pltpu.roll(z, rope_dim, 1)

    krr = matmul(hn, w_kr, b_kr, extras=[(ktab, "col")], epilogue=rope_epilogue, tn=LANES)

    cw = jnp.pad(rg_conv_w[0], ((0, SUBLANES - CONV_W), (0, 0)))
    cb = rg_conv_b[0].reshape(1, d_rnn)
    wa, wx = rg_a_w[0].astype(BF16), rg_x_w[0].astype(BF16)
    ba, bx = rg_a_b[0].reshape(1, d_rnn), rg_x_b[0].reshape(1, d_rnn)
    lam = rg_lambda[0].reshape(1, d_rnn)
    zeros_state = jnp.zeros((nb_p, SUBLANES, d_rnn), F32)
    hy, hl_p = rglru(xr, yg, zeros_state, zeros_state, cw, cb, wa, ba, wx, bx, lam,
                     nseq=nb_p, seqlen=s_p, row0=0)
    conv_s = jnp.pad(state_rg_conv[0], ((0, 0), (SUBLANES - (CONV_W - 1), 0), (0, 0)))
    h0_s = jnp.broadcast_to(state_rg_h[0][:, None, :], (nb_s, SUBLANES, d_rnn)).astype(F32)
    hy, hl_s = rglru(xr, yg, conv_s, h0_s, cw, cb, wa, ba, wx, bx, lam,
                     nseq=nb_s, seqlen=s_s, row0=t_p, hy_prev=hy)
    m1 = matmul(hy, w_rnn_proj[0].astype(BF16), extras=[(sg_rnn, "tile")],
                epilogue=lambda a, g: a * g.astype(F32), out_dtype=BF16)

    def q_epilogue(acc, tab):
        reps = acc.shape[1] // tab.shape[1]
        return acc * (tab if reps == 1 else jnp.concatenate([tab] * reps, axis=1))

    q = matmul(cqn, w_q, extras=[(qtab, "col")], epilogue=q_epilogue, out_dtype=BF16,
               tn=_pick(nheads * 2 * LANES, (1024, 512, 256)))
    kvp = matmul(ckv, w_kv, out_dtype=BF16, m_rows=t_p)
    o = attn_prompt(q, kvp, krr, nbatch=nb_p, seqlen=s_p, nheads=nheads)
    past_kr2 = jnp.concatenate([cache_mla_krope[0], cache_mla_krope[0]], axis=-1)
    o = attn_sample(q, ckv, krr, cache_mla_ckv[0], past_kr2, w_ukt, w_uvh, o,
                    nbatch=nb_s, seqlen=s_s, row0=t_p, nheads=nheads)
    merged = matmul(o, w_mla_proj[0].astype(BF16), extras=[(sg_mla, "tile"), (m1, "tile")],
                    epilogue=lambda a, g, r: a * g.astype(F32) + r.astype(F32), out_dtype=BF16)
    x1 = matmul(merged, w_out[0].astype(BF16), extras=[(x_all, "tile")], epilogue=lambda a, r: a + r)

    xcols = xheads * xhd
    mn = rmsnorm_rows(mem_prompt.reshape(nb_p * n_mem, d), norm_mem_g[0], BF16)
    mk = matmul(mn, w_xk[0].reshape(d, xcols).astype(BF16))
    mv = matmul(mn, w_xv[0].reshape(d, xcols).astype(BF16))
    rw = jnp.pad(router_w[0], ((0, 0), (0, LANES - ne)))
    rwh = rw.astype(BF16)
    rwl = (rw - rwh.astype(F32)).astype(BF16)
    rb = jnp.pad(router_b[0].astype(F32), (0, LANES - ne), constant_values=NEG).reshape(1, LANES)
    wq_x = w_xq[0].reshape(d, xcols).astype(BF16)
    wo_x = w_xo[0].reshape(xcols, d).astype(BF16)
    gx, gf = norm_x_g[0].reshape(1, d), norm_ffn_g[0].reshape(1, d)
    tile_p = _pick(s_p, (512, 256, 128, 64))
    outs = xattn_router(x1, gx, wq_x, mk.reshape(nb_p, n_mem, xcols), mv.reshape(nb_p, n_mem, xcols), wo_x, gf,
                        rwh, rwl, rb, jnp.zeros((SUBLANES, LANES), F32),
                        nb=1, seg=tile_p, ntiles=t_p // tile_p, row0=0, xheads=xheads)
    nb_x = _pick(nb_s, (8, 4, 2, 1))
    outs = xattn_router(x1, gx, wq_x, cache_mem_k[0].reshape(nb_s, n_mem, xcols),
                        cache_mem_v[0].reshape(nb_s, n_mem, xcols), wo_x, gf, rwh, rwl, rb, outs[4],
                        nb=nb_x, seg=s_s, ntiles=nb_s // nb_x, row0=t_p, xheads=xheads, prev=outs[:4])
    x2, hp, ir, gw, cnt = outs

    tme = 512
    counts = cnt[0, :ne].astype(I32)
    nblk_e = (counts + tme - 1) // tme
    offs = ((jnp.cumsum(nblk_e) - nblk_e) * tme).astype(I32)
    n_rows = (t_all * TOP_K // tme + ne) * tme
    dst_t = jnp.transpose(offs[ir[:, :TOP_K]] + ir[:, TOP_K:2 * TOP_K])
    last = jnp.where(nblk_e > 0, offs + (nblk_e - 1) * tme, -1).astype(I32)
    xb = moe_dispatch(last, dst_t, hp, n_rows, tme)
    yb, grp = moe_experts(xb, nblk_e, w_gate[0], b_gate[0], w_up[0], b_up[0], w_down[0], b_down[0], tme=tme)
    y_p = moe_combine(dst_t, gw, x2, norm_final_g, yb, rows=t_p, row0=0, grp=grp)
    y_s = moe_combine(dst_t, gw, x2, norm_final_g, yb, rows=t_s, row0=t_p, grp=grp)

    nconv = CONV_W - 1
    assert s_p >= nconv and s_s >= nconv
    tail = jnp.arange(-nconv, 0, dtype=I32)[None, :]
    rows_p = (jnp.arange(1, nb_p + 1, dtype=I32) * s_p)[:, None] + tail
    rows_s = t_p + (jnp.arange(1, nb_s + 1, dtype=I32) * s_s)[:, None] + tail
    conv_p = jnp.take(xr, rows_p.reshape(-1), axis=0).reshape(nb_p, nconv, d_rnn)
    conv_new_s = jnp.take(xr, rows_s.reshape(-1), axis=0).reshape(nb_s, nconv, d_rnn)
    return (
        y_p.reshape(nb_p, s_p, d), y_s.reshape(nb_s, s_s, d),
        hl_p[None, :, 0, :], conv_p[None],
        ckv[:t_p].reshape(1, nb_p, s_p, kv_lora), krr[:t_p, :rope_dim].reshape(1, nb_p, s_p, rope_dim),
        mk.reshape(1, nb_p, n_mem, xheads, xhd), mv.reshape(1, nb_p, n_mem, xheads, xhd),
        hl_s[None, :, 0, :], conv_new_s[None],
        ckv[t_p:].reshape(1, nb_s, s_s, kv_lora), krr[t_p:, :rope_dim].reshape(1, nb_s, s_s, rope_dim),
    )
```

```python
import functools

import jax
import jax.numpy as jnp
from jax import lax
from jax.experimental import pallas as pl
from jax.experimental.pallas import tpu as pltpu

F32, BF16, I32 = jnp.float32, jnp.bfloat16, jnp.int32

CHUNK = 64
NORM_EPS = 1e-6
RG_C = 8.0
CONV_W = 4
ROPE_BASE = 10000.0
SWIGLU_LIMIT = 7.0
SWIGLU_ALPHA = 1.702
TOP_K = 4
LANES = 128
SUBLANES = 8
VMEM_LIMIT = 56 * 1024 * 1024
DMA_UNROLL = 8
NEG = -1e30
LOG2_E = 1.4426950408889634


def _pick(n, prefs):
    for p in prefs:
        if n % p == 0:
            return p
    raise ValueError(f"no tile for {n} in {prefs}")


def _cparams(*sem):
    return pltpu.CompilerParams(dimension_semantics=sem, vmem_limit_bytes=VMEM_LIMIT)


def _rms(x, g):
    return x * lax.rsqrt(jnp.mean(x * x, axis=-1, keepdims=True) + NORM_EPS) * g


def _dot(a, b):
    return jnp.dot(a, b, preferred_element_type=F32)


def _dot_t(a, b):
    return lax.dot_general(a, b, (((1,), (1,)), ((), ())), preferred_element_type=F32)


def _rmsnorm_kernel(x_ref, g_ref, *rest):
    o_ref = rest[-1]
    o_ref[...] = _rms(x_ref[...].astype(F32), g_ref[...]).astype(o_ref.dtype)


def rmsnorm_rows(x, g, out_dtype, out_rows=None, row0=0, prev=None):
    m, n = x.shape
    tm = _pick(m, (512, 256, 128, 64, 8))
    rb0 = row0 // tm
    in_specs = [pl.BlockSpec((tm, n), lambda i: (i, 0)), pl.BlockSpec((1, n), lambda i: (0, 0))]
    args = [x, g.reshape(1, n).astype(F32)]
    aliases = {}
    if prev is not None:
        in_specs.append(pl.BlockSpec(memory_space=pl.ANY))
        args.append(prev)
        aliases = {2: 0}
    return pl.pallas_call(
        _rmsnorm_kernel,
        grid=(m // tm,),
        in_specs=in_specs,
        out_specs=pl.BlockSpec((tm, n), lambda i: (rb0 + i, 0)),
        out_shape=jax.ShapeDtypeStruct((out_rows or m, n), out_dtype),
        input_output_aliases=aliases,
        name="rmsnorm_rows",
        compiler_params=_cparams("parallel"),
    )(*args)


def _mm_kernel(*refs, has_bias, n_extras, epilogue):
    x_ref, w_ref = refs[0], refs[1]
    acc = _dot(x_ref[...].astype(BF16), w_ref[...])
    k = 2
    if has_bias:
        acc = acc + refs[k][...]
        k += 1
    extras = [r[...] for r in refs[k:k + n_extras]]
    o_ref = refs[-1]
    o_ref[...] = epilogue(acc, *extras).astype(o_ref.dtype)


def matmul(x, w, bias=None, extras=(), epilogue=lambda a: a, out_dtype=F32, m_rows=None, tn=None, name="matmul",
           row0=0, out_rows=None, prev=None):
    m = x.shape[0] if m_rows is None else m_rows
    kdim, n = w.shape
    tm = _pick(m, (1024, 512, 256, 128, 64, 8))
    tn = tn or _pick(n, (1024, 512, 256, 128))
    rb0 = row0 // tm
    in_specs = [pl.BlockSpec((tm, kdim), lambda i, j: (rb0 + i, 0)), pl.BlockSpec((kdim, tn), lambda i, j: (0, j))]
    args = [x, w]
    if bias is not None:
        in_specs.append(pl.BlockSpec((1, tn), lambda i, j: (0, j)))
        args.append(bias.reshape(1, n).astype(F32))
    for arr, kind in extras:
        if kind == "tile":
            in_specs.append(pl.BlockSpec((tm, tn), lambda i, j: (rb0 + i, j)))
        elif kind == "tile_local":
            in_specs.append(pl.BlockSpec((tm, tn), lambda i, j: (i, j)))
        elif kind == "row":
            in_specs.append(pl.BlockSpec((1, tn), lambda i, j: (0, j)))
        else:
            in_specs.append(pl.BlockSpec((tm, arr.shape[1]), lambda i, j: (rb0 + i, 0)))
        args.append(arr)
    aliases = {}
    if prev is not None:
        in_specs.append(pl.BlockSpec(memory_space=pl.ANY))
        args.append(prev)
        aliases = {len(args) - 1: 0}
    return pl.pallas_call(
        functools.partial(_mm_kernel, has_bias=bias is not None, n_extras=len(extras), epilogue=epilogue),
        grid=(m // tm, n // tn),
        in_specs=in_specs,
        out_specs=pl.BlockSpec((tm, tn), lambda i, j: (rb0 + i, j)),
        out_shape=jax.ShapeDtypeStruct((out_rows or m, n), out_dtype),
        input_output_aliases=aliases,
        name=name,
        compiler_params=_cparams("parallel", "parallel"),
    )(*args)


def _rglru_kernel(xr_ref, yg_ref, conv0_ref, h0_ref, cw_ref, cb_ref, wa_ref, ba_ref, wx_ref, bx_ref,
                  lam_ref, *rest, tt, nblk, n_alias):
    hy_ref, hl_ref, cx, ch = rest[n_alias:]
    t = pl.program_id(2)

    @pl.when(t == 0)
    def _():
        cx[0:SUBLANES, :] = conv0_ref[0]
        ch[...] = h0_ref[0]

    cx[SUBLANES:, :] = xr_ref[...]
    cw = cw_ref[...]
    xc = cb_ref[...]
    for k in range(CONV_W):
        off = SUBLANES - (CONV_W - 1) + k
        xc = xc + cx[off:off + tt, :] * cw[k:k + 1]
    cx[0:SUBLANES, :] = cx[tt:tt + SUBLANES, :]

    ra, ia = [], []
    for b in range(nblk):
        xb = xc[:, b * LANES:(b + 1) * LANES].astype(BF16)
        ra.append(_dot(xb, wa_ref[b]))
        ia.append(_dot(xb, wx_ref[b]))
    cat = (lambda v: v[0] if len(v) == 1 else jnp.concatenate(v, axis=1))
    r = jax.nn.sigmoid(cat(ra) + ba_ref[...])
    gi = jax.nn.sigmoid(cat(ia) + bx_ref[...])
    log_a = -RG_C * r * jax.nn.softplus(-lam_ref[...])
    a = jnp.exp(log_a)
    u = jnp.sqrt(-jnp.tanh(log_a) * (1.0 + a * a)) * (gi * xc)

    ngrp, dc = tt // SUBLANES, a.shape[1]
    a = a.reshape(ngrp, SUBLANES, dc)
    u = u.reshape(ngrp, SUBLANES, dc)
    sub = lax.broadcasted_iota(I32, (1, SUBLANES, 1), 1)
    d = 1
    while d < SUBLANES:
        keep = sub >= d
        a_s = jnp.where(keep, pltpu.roll(a, d, 1), 1.0)
        u_s = jnp.where(keep, pltpu.roll(u, d, 1), 0.0)
        u = a * u_s + u
        a = a * a_s
        d *= 2
    last = ch[0:1]
    groups = []
    for g in range(ngrp):
        hg = u[g] + a[g] * last
        groups.append(hg)
        last = hg[SUBLANES - 1:SUBLANES]
    h = jnp.concatenate(groups, axis=0)
    ch[...] = jnp.broadcast_to(last, ch.shape)
    hy_ref[...] = (h * yg_ref[...].astype(F32)).astype(hy_ref.dtype)
    hl_ref[0] = jnp.broadcast_to(last, hl_ref.shape[1:])


def rglru(xr, yg, conv0, h0, cw, cb, wa, ba, wx, bx, lam, *, nseq, seqlen, row0, hy_prev=None):
    t_all, d = xr.shape
    tt = _pick(seqlen, (256, 128, 64))
    nt = seqlen // tt
    dc = _pick(d, (512, 256, 128))
    nblk = dc // LANES
    rb0 = row0 // tt
    rowmap = lambda s, c, t: (rb0 + s * nt + t, c)
    vec = lambda s, c, t: (0, c)
    in_specs = [
        pl.BlockSpec((tt, dc), rowmap), pl.BlockSpec((tt, dc), rowmap),
        pl.BlockSpec((1, SUBLANES, dc), lambda s, c, t: (s, 0, c)),
        pl.BlockSpec((1, SUBLANES, dc), lambda s, c, t: (s, 0, c)),
        pl.BlockSpec((SUBLANES, dc), vec), pl.BlockSpec((1, dc), vec),
        pl.BlockSpec((nblk, LANES, LANES), lambda s, c, t: (c, 0, 0)), pl.BlockSpec((1, dc), vec),
        pl.BlockSpec((nblk, LANES, LANES), lambda s, c, t: (c, 0, 0)), pl.BlockSpec((1, dc), vec),
        pl.BlockSpec((1, dc), vec),
    ]
    args = [xr, yg, conv0, h0, cw, cb, wa, ba, wx, bx, lam]
    aliases = {}
    if hy_prev is not None:
        in_specs.append(pl.BlockSpec(memory_space=pl.ANY))
        args.append(hy_prev)
        aliases = {len(args) - 1: 0}
    return pl.pallas_call(
        functools.partial(_rglru_kernel, tt=tt, nblk=nblk, n_alias=len(aliases)),
        grid=(nseq, d // dc, nt),
        in_specs=in_specs,
        out_specs=[pl.BlockSpec((tt, dc), rowmap), pl.BlockSpec((1, SUBLANES, dc), lambda s, c, t: (s, 0, c))],
        out_shape=[jax.ShapeDtypeStruct((t_all, d), BF16), jax.ShapeDtypeStruct((nseq, SUBLANES, d), F32)],
        scratch_shapes=[pltpu.VMEM((tt + SUBLANES, dc), F32), pltpu.VMEM((SUBLANES, dc), F32)],
        input_output_aliases=aliases,
        name="rglru",
        compiler_params=_cparams("parallel", "parallel", "arbitrary"),
    )(*args)


def _attn_prompt_kernel(q_ref, k_ref, v_ref, kr_ref, o_ref, *, tq):
    qi = pl.program_id(2)
    q = q_ref[...]

    def step(carry, j0, n, mask=None):
        m, l, acc = carry
        j0 = pl.multiple_of(j0, tq)
        k = jnp.concatenate([k_ref[pl.ds(j0, n), :], kr_ref[pl.ds(j0, n), :].astype(BF16)], axis=1)
        s = _dot_t(q, k)
        if mask is not None:
            s = jnp.where(mask, s, NEG)
        m_new = jnp.maximum(m, jnp.max(s, axis=1, keepdims=True))
        alpha = jnp.exp2(m - m_new)
        p = jnp.exp2(s - m_new)
        l = alpha * l + jnp.sum(p, axis=1, keepdims=True)
        acc = alpha * acc + _dot(p.astype(BF16), v_ref[pl.ds(j0, n), :])
        return m_new, l, acc

    init = (jnp.full((tq, 1), NEG, F32), jnp.zeros((tq, 1), F32), jnp.zeros((tq, LANES), F32))
    carry = lax.fori_loop(0, qi // 2, lambda j, c: step(c, j * 2 * tq, 2 * tq), init)
    def visible(n):
        qc = lax.broadcasted_iota(I32, (tq, n), 0) // CHUNK
        kc = lax.broadcasted_iota(I32, (tq, n), 1) // CHUNK
        return kc - (n - tq) // CHUNK <= qc

    def tail_with_earlier_block(c):
        return step(c, (qi - 1) * tq, 2 * tq, mask=visible(2 * tq))

    def tail_diagonal_only(c):
        return step(c, qi * tq, tq, mask=visible(tq))

    _, l, acc = lax.cond(qi % 2 == 1, tail_with_earlier_block, tail_diagonal_only, carry)
    o_ref[...] = (acc / l).astype(o_ref.dtype)


def attn_prompt(q, kv, krr, *, nbatch, seqlen, nheads):
    t_all = q.shape[0]
    tq = _pick(seqlen, (512, 256, 128, 64))
    nq = seqlen // tq
    return pl.pallas_call(
        functools.partial(_attn_prompt_kernel, tq=tq),
        name="attn_prompt",
        grid=(nbatch, nheads, nq),
        in_specs=[
            pl.BlockSpec((tq, 2 * LANES), lambda b, h, i: (b * nq + i, h)),
            pl.BlockSpec((seqlen, LANES), lambda b, h, i: (b, h)),
            pl.BlockSpec((seqlen, LANES), lambda b, h, i: (b, nheads + h)),
            pl.BlockSpec((seqlen, LANES), lambda b, h, i: (b, 0)),
        ],
        out_specs=pl.BlockSpec((tq, LANES), lambda b, h, i: (b * nq + i, h)),
        out_shape=jax.ShapeDtypeStruct((t_all, nheads * LANES), BF16),
        compiler_params=_cparams("parallel", "parallel", "arbitrary"),
    )(q, kv, kv, krr)


def _attn_sample_kernel(q_ref, cn_ref, kr_ref, past_ref, pkr_ref, wukt_ref, wuv_ref, o_prev, o_ref, *, nheads):
    del o_prev
    q = q_ref[...]
    s_len = q.shape[0]
    lat = jnp.concatenate([past_ref[0].astype(BF16), cn_ref[...].astype(BF16)], axis=0)
    kro = jnp.concatenate([pkr_ref[0].astype(BF16), kr_ref[...].astype(BF16)], axis=0)
    kext = jnp.concatenate([lat, kro], axis=1)
    qs = []
    for h in range(nheads):
        qn = q[:, h * 2 * LANES:h * 2 * LANES + LANES]
        qr = q[:, h * 2 * LANES + LANES:(h + 1) * 2 * LANES]
        qs.append(jnp.concatenate([_dot(qn, wukt_ref[h]).astype(BF16), qr], axis=1))
    qext = jnp.concatenate(qs, axis=0)
    s = _dot_t(qext, kext)
    m = jnp.max(s, axis=1, keepdims=True)
    p = jnp.exp2(s - m)
    l = jnp.sum(p, axis=1, keepdims=True)
    ol = (_dot(p.astype(BF16), lat) / l).astype(BF16)
    outs = [_dot(ol[h * s_len:(h + 1) * s_len], wuv_ref[h]) for h in range(nheads)]
    o_ref[...] = jnp.concatenate(outs, axis=1).astype(o_ref.dtype)


def attn_sample(q, ckv, krr, past_ckv, past_kr2, wukt, wuv, o_prev, *, nbatch, seqlen, row0, nheads):
    t_all = q.shape[0]
    rb0 = row0 // seqlen
    p_len, c = past_ckv.shape[1:]
    return pl.pallas_call(
        functools.partial(_attn_sample_kernel, nheads=nheads),
        grid=(nbatch,),
        in_specs=[
            pl.BlockSpec((seqlen, nheads * 2 * LANES), lambda b: (rb0 + b, 0)),
            pl.BlockSpec((seqlen, c), lambda b: (rb0 + b, 0)),
            pl.BlockSpec((seqlen, LANES), lambda b: (rb0 + b, 0)),
            pl.BlockSpec((1, p_len, c), lambda b: (b, 0, 0)),
            pl.BlockSpec((1, p_len, LANES), lambda b: (b, 0, 0)),
            pl.BlockSpec(wukt.shape, lambda b: (0, 0, 0)),
            pl.BlockSpec(wuv.shape, lambda b: (0, 0, 0)),
            pl.BlockSpec(memory_space=pl.ANY),
        ],
        out_specs=pl.BlockSpec((seqlen, nheads * LANES), lambda b: (rb0 + b, 0)),
        out_shape=jax.ShapeDtypeStruct((t_all, nheads * LANES), BF16),
        input_output_aliases={7: 0},
        name="attn_sample",
        compiler_params=_cparams("parallel"),
    )(q, ckv, krr, past_ckv, past_kr2, wukt, wuv, o_prev)


def _xattn_kernel(x1_ref, gx_ref, wq_ref, mk_ref, mv_ref, wo_ref, gf_ref, rwh_ref, rwl_ref, rb_ref, cnt0_ref,
                  *rest, nb, seg, xheads, n_alias):
    x2_ref, hp_ref, ir_ref, gw_ref, cnt_ref, carry = rest[n_alias:]
    i = pl.program_id(0)

    @pl.when(i == 0)
    def _():
        carry[...] = cnt0_ref[...]

    x1 = x1_ref[...]
    tm = x1.shape[0]
    hn = _rms(x1, gx_ref[...]).astype(BF16)
    q = _dot(hn, wq_ref[...]).astype(BF16)
    scale = LANES ** -0.5
    segs = []
    for n in range(nb):
        heads = []
        for h in range(xheads):
            cols = slice(h * LANES, (h + 1) * LANES)
            qh = q[n * seg:(n + 1) * seg, cols]
            kh = mk_ref[n][:, cols].astype(BF16)
            vh = mv_ref[n][:, cols].astype(BF16)
            s = _dot_t(qh, kh) * scale
            m = jnp.max(s, axis=1, keepdims=True)
            p = jnp.exp(s - m)
            l = jnp.sum(p, axis=1, keepdims=True)
            heads.append(_dot(p.astype(BF16), vh) / l)
        segs.append(jnp.concatenate(heads, axis=1))
    o = (segs[0] if nb == 1 else jnp.concatenate(segs, axis=0)).astype(BF16)
    x2 = x1 + _dot(o, wo_ref[...])
    x2_ref[...] = x2

    hf = _rms(x2, gf_ref[...])
    hp_ref[...] = hf
    hb = hf.astype(BF16)
    hl = (hf - hb.astype(F32)).astype(BF16)
    logits = _dot(hb, rwh_ref[...]) + _dot(hl, rwh_ref[...]) + _dot(hb, rwl_ref[...]) + rb_ref[...]

    lane = lax.broadcasted_iota(I32, (tm, LANES), 1)
    work = logits
    sels, vals, idxs = [], [], []
    for _ in range(TOP_K):
        m = jnp.max(work, axis=1, keepdims=True)
        idx = jnp.min(jnp.where(work == m, lane, LANES), axis=1, keepdims=True)
        sel = lane == idx
        sels.append(sel)
        vals.append(m)
        idxs.append(idx)
        work = jnp.where(sel, -jnp.inf, work)
    es = [jnp.exp(v - vals[0]) for v in vals]
    den = es[0] + es[1] + es[2] + es[3]
    onehot = (sels[0] | sels[1] | sels[2] | sels[3]).astype(F32)
    tri = (lax.broadcasted_iota(I32, (tm, tm), 0) > lax.broadcasted_iota(I32, (tm, tm), 1)).astype(BF16)
    excl = _dot(tri, onehot.astype(BF16)) + carry[0:1]
    carry[...] = carry[...] + jnp.sum(onehot, axis=0, keepdims=True)
    ir = jnp.zeros((tm, LANES), I32)
    gw = jnp.zeros((tm, LANES), F32)
    for k in range(TOP_K):
        rank = jnp.sum(jnp.where(sels[k], excl, 0.0), axis=1, keepdims=True).astype(I32)
        ir = jnp.where(lane == k, idxs[k], ir)
        ir = jnp.where(lane == TOP_K + k, rank, ir)
        gw = jnp.where(lane == k, es[k] / den, gw)
    ir_ref[...] = ir
    gw_ref[...] = gw
    cnt_ref[...] = carry[...]


def xattn_router(x1, gx, wq, mk, mv, wo, gf, rwh, rwl, rb, cnt0, *, nb, seg, ntiles, row0, xheads, prev=None):
    t_all, d = x1.shape
    tm = nb * seg
    rb0 = row0 // tm
    rows = lambda i: (rb0 + i, 0)
    const = lambda i: (0, 0)
    mem_rows, mem_cols = mk.shape[1:]
    if nb == 1:
        tiles_per_batch = ntiles // mk.shape[0]
        mem_map = lambda i: (i // tiles_per_batch, 0, 0)
    else:
        mem_map = lambda i: (i, 0, 0)
    in_specs = [
        pl.BlockSpec((tm, d), rows), pl.BlockSpec((1, d), const), pl.BlockSpec(wq.shape, const),
        pl.BlockSpec((nb, mem_rows, mem_cols), mem_map), pl.BlockSpec((nb, mem_rows, mem_cols), mem_map),
        pl.BlockSpec(wo.shape, const), pl.BlockSpec((1, d), const),
        pl.BlockSpec(rwh.shape, const), pl.BlockSpec(rwl.shape, const), pl.BlockSpec((1, LANES), const),
        pl.BlockSpec((SUBLANES, LANES), const),
    ]
    args = [x1, gx, wq, mk, mv, wo, gf, rwh, rwl, rb, cnt0]
    aliases = {}
    if prev is not None:
        for k, arr in enumerate(prev):
            in_specs.append(pl.BlockSpec(memory_space=pl.ANY))
            args.append(arr)
            aliases[len(args) - 1] = k
    out_shape = [
        jax.ShapeDtypeStruct((t_all, d), F32), jax.ShapeDtypeStruct((t_all, d), F32),
        jax.ShapeDtypeStruct((t_all, LANES), I32), jax.ShapeDtypeStruct((t_all, LANES), F32),
        jax.ShapeDtypeStruct((SUBLANES, LANES), F32),
    ]
    out_specs = [
        pl.BlockSpec((tm, d), rows), pl.BlockSpec((tm, d), rows),
        pl.BlockSpec((tm, LANES), rows), pl.BlockSpec((tm, LANES), rows),
        pl.BlockSpec((SUBLANES, LANES), const),
    ]
    return pl.pallas_call(
        functools.partial(_xattn_kernel, nb=nb, seg=seg, xheads=xheads, n_alias=len(aliases)),
        grid=(ntiles,),
        in_specs=in_specs,
        out_specs=out_specs,
        out_shape=out_shape,
        scratch_shapes=[pltpu.VMEM((SUBLANES, LANES), F32)],
        input_output_aliases=aliases,
        name="xattn_router",
        compiler_params=_cparams("arbitrary"),
    )(*args)


def _row_copy(src, dst, i, j, sem):
    return pltpu.make_async_copy(src.at[pl.ds(i, 1)], dst.at[pl.ds(j, 1)], sem)


def _token_row_copies(n_tokens, copy, same_size_copy):
    def issue(t, c):
        for k in range(TOP_K):
            copy(t, k).start(priority=k % 2)
        return c

    lax.fori_loop(0, n_tokens, issue, 0, unroll=DMA_UNROLL)

    def drain(t, c):
        for _ in range(TOP_K):
            same_size_copy.wait()
        return c

    lax.fori_loop(0, n_tokens, drain, 0, unroll=DMA_UNROLL)


def _dispatch_kernel(last_ref, dst_ref, h_ref, xb_out, zbuf, sem, *, tmd, tme, ne):
    @pl.when(pl.program_id(0) == 0)
    def _():
        zbuf[...] = jnp.zeros_like(zbuf)

        def block_copy(e):
            return pltpu.make_async_copy(
                zbuf, xb_out.at[pl.ds(pl.multiple_of(last_ref[e], tme), tme)], sem.at[0])

        def fill(e, c):
            @pl.when(last_ref[e] >= 0)
            def _():
                block_copy(e).start()
            return c

        def drain(e, c):
            @pl.when(last_ref[e] >= 0)
            def _():
                block_copy(e).wait()
            return c

        lax.fori_loop(0, ne, fill, 0)
        lax.fori_loop(0, ne, drain, 0)

    def copy(t, k):
        return _row_copy(h_ref, xb_out, t, dst_ref[k, t], sem.at[0])

    _token_row_copies(tmd, copy, _row_copy(h_ref, xb_out, 0, 0, sem.at[0]))


def moe_dispatch(last, dst_t, hp, n_rows, tme):
    t_all, w = hp.shape
    tmd = _pick(t_all, (512, 256, 128))
    return pl.pallas_call(
        functools.partial(_dispatch_kernel, tmd=tmd, tme=tme, ne=last.shape[0]),
        grid_spec=pltpu.PrefetchScalarGridSpec(
            num_scalar_prefetch=1,
            grid=(t_all // tmd,),
            in_specs=[
                pl.BlockSpec((TOP_K, tmd), lambda i, last: (0, i), memory_space=pltpu.SMEM),
                pl.BlockSpec((tmd, w), lambda i, last: (i, 0)),
            ],
            out_specs=pl.BlockSpec(memory_space=pl.ANY),
            scratch_shapes=[pltpu.VMEM((tme, w), hp.dtype), pltpu.SemaphoreType.DMA((1,))],
        ),
        out_shape=jax.ShapeDtypeStruct((n_rows, w), hp.dtype),
        name="moe_dispatch",
        compiler_params=_cparams("arbitrary"),
    )(last, dst_t, hp)


def _expert_up_kernel(se, sj, si, sf, sv, xb_ref, wg_ref, wu_ref, bg_ref, bu_ref, o_ref, wgb, wub):
    s = pl.program_id(0)

    @pl.when(sf[s] == 1)
    def _():
        wgb[...] = wg_ref[0].astype(BF16)
        wub[...] = wu_ref[0].astype(BF16)

    @pl.when(sv[s] == 1)
    def _():
        x = xb_ref[...].astype(BF16)
        g = jnp.minimum(_dot(x, wgb[...]) + bg_ref[0], SWIGLU_LIMIT)
        u = jnp.clip(_dot(x, wub[...]) + bu_ref[0], -SWIGLU_LIMIT, SWIGLU_LIMIT)
        o_ref[...] = ((u + 1.0) * g * jax.nn.sigmoid(SWIGLU_ALPHA * g)).astype(o_ref.dtype)


def _expert_down_kernel(se, sj, si, sf, sv, a_ref, wd_ref, bd_ref, o_ref, wdb):
    s = pl.program_id(0)

    @pl.when(sf[s] == 1)
    def _():
        wdb[...] = wd_ref[0].astype(BF16)

    @pl.when(sv[s] == 1)
    def _():
        y = _dot(a_ref[...], wdb[...]) + bd_ref[0]
        o_ref[...] = y


def _expert_schedule(nblk_e, nj, nb_max):
    cum = jnp.cumsum(nblk_e)
    start = cum - nblk_e
    total = cum[-1]
    ns = nj * nb_max
    s = jnp.clip(jnp.arange(ns, dtype=I32), 0, jnp.maximum(nj * total - 1, 0))
    e = jnp.sum((s[:, None] >= nj * cum[None, :]).astype(I32), axis=1)
    local = s - nj * start[e]
    n_e = jnp.maximum(nblk_e[e], 1)
    j = local // n_e
    r = local % n_e
    valid = (jnp.arange(ns, dtype=I32) < nj * total).astype(I32)
    first = ((r == 0).astype(I32)) * valid
    return e, j.astype(I32), (start[e] + r).astype(I32), first, valid


def moe_experts(xb, nblk_e, w_gate, b_gate, w_up, b_up, w_down, b_down, *, tme):
    n_rows, d = xb.shape
    ne, _, dff = w_gate.shape
    nb_max = n_rows // tme
    tn_up = _pick(dff, (1024, 512, 256, 128))
    nj = dff // tn_up
    sched = _expert_schedule(nblk_e, nj, nb_max)
    act = pl.pallas_call(
        _expert_up_kernel,
        grid_spec=pltpu.PrefetchScalarGridSpec(
            num_scalar_prefetch=5,
            grid=(nj * nb_max,),
            in_specs=[
                pl.BlockSpec((tme, d), lambda s, se, sj, si, sf, sv: (si[s], 0)),
                pl.BlockSpec((1, d, tn_up), lambda s, se, sj, si, sf, sv: (se[s], 0, sj[s])),
                pl.BlockSpec((1, d, tn_up), lambda s, se, sj, si, sf, sv: (se[s], 0, sj[s])),
                pl.BlockSpec((1, 1, tn_up), lambda s, se, sj, si, sf, sv: (se[s], 0, sj[s])),
                pl.BlockSpec((1, 1, tn_up), lambda s, se, sj, si, sf, sv: (se[s], 0, sj[s])),
            ],
            out_specs=pl.BlockSpec((tme, tn_up), lambda s, se, sj, si, sf, sv: (si[s], sj[s])),
            scratch_shapes=[pltpu.VMEM((d, tn_up), BF16), pltpu.VMEM((d, tn_up), BF16)],
        ),
        out_shape=jax.ShapeDtypeStruct((n_rows, dff), BF16),
        name="expert_up",
        compiler_params=_cparams("arbitrary"),
    )(*sched, xb, w_gate, w_up, b_gate.reshape(ne, 1, dff), b_up.reshape(ne, 1, dff))

    tn_dn = _pick(d, (1024, 512, 256))
    nj2 = d // tn_dn
    sched2 = _expert_schedule(nblk_e, nj2, nb_max)
    yb = pl.pallas_call(
        _expert_down_kernel,
        grid_spec=pltpu.PrefetchScalarGridSpec(
            num_scalar_prefetch=5,
            grid=(nj2 * nb_max,),
            in_specs=[
                pl.BlockSpec((tme, dff), lambda s, se, sj, si, sf, sv: (si[s], 0)),
                pl.BlockSpec((1, dff, tn_dn), lambda s, se, sj, si, sf, sv: (se[s], 0, sj[s])),
                pl.BlockSpec((1, 1, tn_dn), lambda s, se, sj, si, sf, sv: (se[s], 0, sj[s])),
            ],
            out_specs=pl.BlockSpec((tme, tn_dn), lambda s, se, sj, si, sf, sv: (si[s], sj[s])),
            scratch_shapes=[pltpu.VMEM((dff, tn_dn), BF16)],
        ),
        out_shape=jax.ShapeDtypeStruct((n_rows, d), F32),
        name="expert_down",
        compiler_params=_cparams("arbitrary"),
    )(*sched2, act, w_down, b_down.reshape(ne, 1, d))
    return yb


def _combine_kernel(dst_ref, gw_ref, x2_ref, g_ref, yb_hbm, o_ref, buf, sem, *, tmc):
    def copy(t, k):
        return _row_copy(yb_hbm, buf.at[k], dst_ref[k, t], t, sem.at[0])

    _token_row_copies(tmc, copy, _row_copy(yb_hbm, buf.at[0], 0, 0, sem.at[0]))
    y = x2_ref[...]
    gw = gw_ref[...]
    for k in range(TOP_K):
        y = y + gw[:, k:k + 1] * buf[k]
    o_ref[...] = _rms(y, g_ref[...])


def moe_combine(dst_t, gw, x2, g, yb, *, rows, row0):
    d = x2.shape[1]
    tmc = _pick(rows, (256, 128))
    rb0 = row0 // tmc
    return pl.pallas_call(
        functools.partial(_combine_kernel, tmc=tmc),
        grid=(rows // tmc,),
        in_specs=[
            pl.BlockSpec((TOP_K, tmc), lambda i: (0, rb0 + i), memory_space=pltpu.SMEM),
            pl.BlockSpec((tmc, LANES), lambda i: (rb0 + i, 0)),
            pl.BlockSpec((tmc, d), lambda i: (rb0 + i, 0)),
            pl.BlockSpec((1, d), lambda i: (0, 0)),
            pl.BlockSpec(memory_space=pl.ANY),
        ],
        out_specs=pl.BlockSpec((tmc, d), lambda i: (i, 0)),
        scratch_shapes=[pltpu.VMEM((TOP_K, tmc, d), F32), pltpu.SemaphoreType.DMA((1,))],
        out_shape=jax.ShapeDtypeStruct((rows, d), F32),
        name="moe_combine",
        compiler_params=_cparams("arbitrary"),
    )(dst_t, gw, x2, g.reshape(1, d).astype(F32), yb)


def _rot_cols(w):
    half = w.shape[-1] // 2
    return jnp.concatenate([-w[..., half:], w[..., :half]], axis=-1)


def kernel(x_prompt, x_sample, mem_prompt, cache_mla_ckv, cache_mla_krope, cache_mem_k, cache_mem_v, state_rg_h, state_rg_conv, norm_mix_g, w_in, b_in, rg_conv_w, rg_conv_b, rg_a_w, rg_a_b, rg_x_w, rg_x_b, rg_lambda, w_rnn_proj, mla_q_norm_g, mla_w_uq_nope, mla_w_uq_rope, mla_kv_norm_g, mla_w_uk, mla_w_uv, w_mla_proj, w_out, norm_x_g, norm_mem_g, w_xq, w_xk, w_xv, w_xo, norm_ffn_g, router_w, router_b, w_gate, b_gate, w_up, b_up, w_down, b_down, norm_final_g):
    nb_p, s_p, d = x_prompt.shape
    nb_s, s_s, _ = x_sample.shape
    assert w_in.shape[0] == 1, "single-layer step"
    t_p, t_s = nb_p * s_p, nb_s * s_s
    t_all = t_p + t_s
    d_rnn = rg_lambda.shape[-1]
    q_lora, kv_lora = mla_q_norm_g.shape[-1], mla_kv_norm_g.shape[-1]
    nheads, qk_nope = mla_w_uk.shape[2:]
    rope_dim = cache_mla_krope.shape[-1]
    v_dim = mla_w_uv.shape[-1]
    p_len = cache_mla_ckv.shape[2]
    n_mem, xheads, xhd = cache_mem_k.shape[2:]
    ne = router_w.shape[-1]
    assert qk_nope == LANES and v_dim == LANES and xhd == LANES and 2 * rope_dim == LANES
    assert p_len % CHUNK == 0 and s_s <= CHUNK and ne <= LANES

    wi, bi = w_in[0], b_in[0]
    o1, o2 = d_rnn, 2 * d_rnn
    o3, o4 = o2 + q_lora, o2 + q_lora + kv_lora
    o5 = o4 + rope_dim
    o6 = o5 + d
    seg = lambda a, b: (wi[:, a:b].astype(BF16), bi[a:b])
    w_xr, b_xr = seg(0, o1)
    w_yr, b_yr = seg(o1, o2)
    w_cq, b_cq = seg(o2, o3)
    w_ckv, b_ckv = seg(o3, o4)
    w_gr, b_gr = seg(o5, o6)
    w_gm, b_gm = seg(o6, o6 + d)
    w_kr = jnp.concatenate([wi[:, o4:o5], _rot_cols(wi[:, o4:o5])], axis=1).astype(BF16)
    b_kr = jnp.concatenate([bi[o4:o5], _rot_cols(bi[o4:o5])])
    w_q = jnp.concatenate([mla_w_uq_nope[0], mla_w_uq_rope[0], _rot_cols(mla_w_uq_rope[0])], axis=-1)
    w_q = w_q.reshape(q_lora, nheads * 2 * LANES).astype(BF16)
    w_kv = jnp.concatenate([mla_w_uk[0].reshape(kv_lora, -1), mla_w_uv[0].reshape(kv_lora, -1)], axis=1).astype(BF16)
    w_ukt = jnp.transpose(mla_w_uk[0], (1, 2, 0)).astype(BF16)
    w_uvh = jnp.transpose(mla_w_uv[0], (1, 0, 2)).astype(BF16)

    half = rope_dim // 2
    freq = ROPE_BASE ** (-jnp.arange(half, dtype=F32) / half)
    pos = jnp.concatenate([jnp.tile(jnp.arange(s_p, dtype=I32), nb_p),
                           jnp.tile(p_len + jnp.arange(s_s, dtype=I32), nb_s)]).astype(F32)
    ang = pos[:, None] * freq[None, :]
    cs, sn = jnp.cos(ang), jnp.sin(ang)
    ktab = jnp.concatenate([cs, cs, sn, sn], axis=1)
    q_scale = float(qk_nope + rope_dim) ** -0.5 * LOG2_E
    qtab = jnp.concatenate([jnp.ones((t_all, LANES), F32), ktab], axis=1) * q_scale

    x_p2, x_s2 = x_prompt.reshape(t_p, d), x_sample.reshape(t_s, d)
    hn = rmsnorm_rows(x_p2, norm_mix_g[0], BF16, out_rows=t_all)
    hn = rmsnorm_rows(x_s2, norm_mix_g[0], BF16, out_rows=t_all, row0=t_p, prev=hn)
    xr = matmul(hn, w_xr, b_xr)
    yg = matmul(hn, w_yr, b_yr, epilogue=jax.nn.gelu, out_dtype=BF16)
    sg_rnn = matmul(hn, w_gr, b_gr, epilogue=jax.nn.sigmoid, out_dtype=BF16)
    sg_mla = matmul(hn, w_gm, b_gm, epilogue=jax.nn.sigmoid, out_dtype=BF16)
    cqn = matmul(hn, w_cq, b_cq, extras=[(mla_q_norm_g[0].reshape(1, -1), "row")], epilogue=_rms,
                 out_dtype=BF16, tn=q_lora)
    ckv = matmul(hn, w_ckv, b_ckv, extras=[(mla_kv_norm_g[0].reshape(1, -1), "row")], epilogue=_rms, tn=kv_lora)

    def rope_epilogue(acc, tab):
        z = acc * tab
        return z + pltpu.roll(z, rope_dim, 1)

    krr = matmul(hn, w_kr, b_kr, extras=[(ktab, "col")], epilogue=rope_epilogue, tn=LANES)

    cw = jnp.pad(rg_conv_w[0], ((0, SUBLANES - CONV_W), (0, 0)))
    cb = rg_conv_b[0].reshape(1, d_rnn)
    wa, wx = rg_a_w[0].astype(BF16), rg_x_w[0].astype(BF16)
    ba, bx = rg_a_b[0].reshape(1, d_rnn), rg_x_b[0].reshape(1, d_rnn)
    lam = rg_lambda[0].reshape(1, d_rnn)
    zeros_state = jnp.zeros((nb_p, SUBLANES, d_rnn), F32)
    hy, hl_p = rglru(xr, yg, zeros_state, zeros_state, cw, cb, wa, ba, wx, bx, lam,
                     nseq=nb_p, seqlen=s_p, row0=0)
    conv_s = jnp.pad(state_rg_conv[0], ((0, 0), (SUBLANES - (CONV_W - 1), 0), (0, 0)))
    h0_s = jnp.broadcast_to(state_rg_h[0][:, None, :], (nb_s, SUBLANES, d_rnn)).astype(F32)
    hy, hl_s = rglru(xr, yg, conv_s, h0_s, cw, cb, wa, ba, wx, bx, lam,
                     nseq=nb_s, seqlen=s_s, row0=t_p, hy_prev=hy)
    m1 = matmul(hy, w_rnn_proj[0].astype(BF16), extras=[(sg_rnn, "tile")],
                epilogue=lambda a, g: a * g.astype(F32), out_dtype=BF16)

    def q_epilogue(acc, tab):
        reps = acc.shape[1] // tab.shape[1]
        return acc * (tab if reps == 1 else jnp.concatenate([tab] * reps, axis=1))

    q = matmul(cqn, w_q, extras=[(qtab, "col")], epilogue=q_epilogue, out_dtype=BF16,
               tn=_pick(nheads * 2 * LANES, (1024, 512, 256)))
    kvp = matmul(ckv, w_kv, out_dtype=BF16, m_rows=t_p)
    o = attn_prompt(q, kvp, krr, nbatch=nb_p, seqlen=s_p, nheads=nheads)
    past_kr2 = jnp.concatenate([cache_mla_krope[0], cache_mla_krope[0]], axis=-1)
    o = attn_sample(q, ckv, krr, cache_mla_ckv[0], past_kr2, w_ukt, w_uvh, o,
                    nbatch=nb_s, seqlen=s_s, row0=t_p, nheads=nheads)
    merged = matmul(o, w_mla_proj[0].astype(BF16), extras=[(sg_mla, "tile"), (m1, "tile")],
                    epilogue=lambda a, g, r: a * g.astype(F32) + r.astype(F32), out_dtype=BF16)
    w_out_b = w_out[0].astype(BF16)
    residual = lambda a, r: a + r
    x1 = matmul(merged, w_out_b, extras=[(x_p2, "tile_local")], epilogue=residual, m_rows=t_p, out_rows=t_all)
    x1 = matmul(merged, w_out_b, extras=[(x_s2, "tile_local")], epilogue=residual, m_rows=t_s, out_rows=t_all,
                row0=t_p, prev=x1)

    xcols = xheads * xhd
    mn = rmsnorm_rows(mem_prompt.reshape(nb_p * n_mem, d), norm_mem_g[0], BF16)
    mk = matmul(mn, w_xk[0].reshape(d, xcols).astype(BF16))
    mv = matmul(mn, w_xv[0].reshape(d, xcols).astype(BF16))
    rw = jnp.pad(router_w[0], ((0, 0), (0, LANES - ne)))
    rwh = rw.astype(BF16)
    rwl = (rw - rwh.astype(F32)).astype(BF16)
    rb = jnp.pad(router_b[0].astype(F32), (0, LANES - ne), constant_values=NEG).reshape(1, LANES)
    wq_x = w_xq[0].reshape(d, xcols).astype(BF16)
    wo_x = w_xo[0].reshape(xcols, d).astype(BF16)
    gx, gf = norm_x_g[0].reshape(1, d), norm_ffn_g[0].reshape(1, d)
    tile_p = _pick(s_p, (512, 256, 128, 64))
    outs = xattn_router(x1, gx, wq_x, mk.reshape(nb_p, n_mem, xcols), mv.reshape(nb_p, n_mem, xcols), wo_x, gf,
                        rwh, rwl, rb, jnp.zeros((SUBLANES, LANES), F32),
                        nb=1, seg=tile_p, ntiles=t_p // tile_p, row0=0, xheads=xheads)
    nb_x = _pick(nb_s, (8, 4, 2, 1))
    outs = xattn_router(x1, gx, wq_x, cache_mem_k[0].reshape(nb_s, n_mem, xcols),
                        cache_mem_v[0].reshape(nb_s, n_mem, xcols), wo_x, gf, rwh, rwl, rb, outs[4],
                        nb=nb_x, seg=s_s, ntiles=nb_s // nb_x, row0=t_p, xheads=xheads, prev=outs[:4])
    x2, hp, ir, gw, cnt = outs

    tme = 512
    counts = cnt[0, :ne].astype(I32)
    nblk_e = (counts + tme - 1) // tme
    offs = ((jnp.cumsum(nblk_e) - nblk_e) * tme).astype(I32)
    n_rows = (t_all * TOP_K // tme + ne) * tme
    dst_t = jnp.transpose(offs[ir[:, :TOP_K]] + ir[:, TOP_K:2 * TOP_K])
    last = jnp.where(nblk_e > 0, offs + (nblk_e - 1) * tme, -1).astype(I32)
    xb = moe_dispatch(last, dst_t, hp, n_rows, tme)
    yb = moe_experts(xb, nblk_e, w_gate[0], b_gate[0], w_up[0], b_up[0], w_down[0], b_down[0], tme=tme)
    y_p = moe_combine(dst_t, gw, x2, norm_final_g, yb, rows=t_p, row0=0)
    y_s = moe_combine(dst_t, gw, x2, norm_final_g, yb, rows=t_s, row0=t_p)

    nconv = CONV_W - 1
    assert s_p >= nconv and s_s >= nconv
    tail = jnp.arange(-nconv, 0, dtype=I32)[None, :]
    rows_p = (jnp.arange(1, nb_p + 1, dtype=I32) * s_p)[:, None] + tail
    rows_s = t_p + (jnp.arange(1, nb_s + 1, dtype=I32) * s_s)[:, None] + tail
    conv_p = jnp.take(xr, rows_p.reshape(-1), axis=0).reshape(nb_p, nconv, d_rnn)
    conv_new_s = jnp.take(xr, rows_s.reshape(-1), axis=0).reshape(nb_s, nconv, d_rnn)
    return (
        y_p.reshape(nb_p, s_p, d), y_s.reshape(nb_s, s_s, d),
        hl_p[None, :, 0, :], conv_p[None],
        ckv[:t_p].reshape(1, nb_p, s_p, kv_lora), krr[:t_p, :rope_dim].reshape(1, nb_p, s_p, rope_dim),
        mk.reshape(1, nb_p, n_mem, xheads, xhd), mv.reshape(1, nb_p, n_mem, xheads, xhd),
        hl_s[None, :, 0, :], conv_new_s[None],
        ckv[t_p:].reshape(1, nb_s, s_s, kv_lora), krr[t_p:, :rope_dim].reshape(1, nb_s, s_s, rope_dim),
    )
```
